```python
import math
import jax
import jax.numpy as jnp
from jax import lax
import numpy as np

D_MODEL = 2048
BATCH = 2
SEQ = 8192
DEPTH = 4

N_MIXERS = 3
MIX_WIDTH = D_MODEL
HEAD_DIM = 128
MEM_LEN = 256
MEM_HEADS = 4
MEM_WIDTH = MEM_HEADS * HEAD_DIM
MIXER_WIDTH = MIX_WIDTH - MEM_WIDTH
D_FF = 4 * D_MODEL
EPS = 1e-6
S5_GROUP = 16
S5_GROUPS = MIXER_WIDTH // S5_GROUP
S5_STATE = 64
S5_CHUNK = 128
GDN_HEADS = MIXER_WIDTH // HEAD_DIM
GDN_CONV = 4
GDN_CHUNK = 64
FOX_HEADS = MIXER_WIDTH // HEAD_DIM
FOX_BLOCK = 128
N_S5 = (DEPTH + 2) // N_MIXERS
N_GDN = (DEPTH + 1) // N_MIXERS
N_FOX = DEPTH // N_MIXERS
S5_IN = MIXER_WIDTH + MEM_WIDTH
GDN_IN = 4 * MIXER_WIDTH + 2 * GDN_HEADS + MEM_WIDTH
FOX_IN = 3 * MIXER_WIDTH + FOX_HEADS + MEM_WIDTH

kernel_name = 'hybrid_s5_gdn_fox_memory_trunk'


def _rmsnorm(x, gain):
    x32 = x.astype(jnp.float32)
    y = x32 * lax.rsqrt(jnp.mean(x32 * x32, axis=-1, keepdims=True) + EPS)
    return (y * gain.astype(jnp.float32)).astype(x.dtype)


def _l2norm(x):
    return x * lax.rsqrt(jnp.sum(x * x, axis=-1, keepdims=True) + EPS)


def _complex_affine_combine(e1, e2):
    a1r, a1i, b1r, b1i = e1
    a2r, a2i, b2r, b2i = e2
    return (a1r * a2r - a1i * a2i,
            a1r * a2i + a1i * a2r,
            a2r * b1r - a2i * b1i + b2r,
            a2r * b1i + a2i * b1r + b2i)


def _s5_mixer(u, lam_re, lam_im, log_dt, b_re, b_im, c_re, c_im, d_skip, w_glu, b_glu):
    bsz, seq, _ = u.shape
    f32 = jnp.float32
    n_chunks = seq // S5_CHUNK
    l_re, l_im = lam_re.astype(f32), lam_im.astype(f32)
    dt = jnp.exp(log_dt.astype(f32))[:, None]
    mag = jnp.exp(l_re * dt)
    a_re, a_im = mag * jnp.cos(l_im * dt), mag * jnp.sin(l_im * dt)
    den = l_re * l_re + l_im * l_im
    z_re = ((a_re - 1.0) * l_re + a_im * l_im) / den
    z_im = (a_im * l_re - (a_re - 1.0) * l_im) / den
    br, bi = b_re.astype(f32), b_im.astype(f32)
    bb_re = z_re[..., None] * br - z_im[..., None] * bi
    bb_im = z_re[..., None] * bi + z_im[..., None] * br
    cr, ci = c_re.astype(f32), c_im.astype(f32)
    blk = (bsz, S5_CHUNK, S5_GROUPS, S5_STATE)
    a_blk_re, a_blk_im = jnp.broadcast_to(a_re, blk), jnp.broadcast_to(a_im, blk)
    u32 = u.astype(f32)
    u_chunks = u32.reshape(bsz, n_chunks, S5_CHUNK, S5_GROUPS, S5_GROUP).transpose(1, 0, 2, 3, 4)

    def step(carry, uc):
        s_re, s_im = carry
        bu_re = jnp.einsum('blgc,gpc->blgp', uc, bb_re)
        bu_im = jnp.einsum('blgc,gpc->blgp', uc, bb_im)
        p_re, p_im, h_re, h_im = lax.associative_scan(
            _complex_affine_combine, (a_blk_re, a_blk_im, bu_re, bu_im), axis=1)
        h_re = h_re + p_re * s_re[:, None] - p_im * s_im[:, None]
        h_im = h_im + p_re * s_im[:, None] + p_im * s_re[:, None]
        y = jnp.einsum('blgp,gcp->blgc', h_re, cr) - jnp.einsum('blgp,gcp->blgc', h_im, ci)
        return (h_re[:, -1], h_im[:, -1]), y

    zeros = jnp.zeros((bsz, S5_GROUPS, S5_STATE), f32)
    _, y = lax.scan(step, (zeros, zeros), u_chunks)
    y = y.transpose(1, 0, 2, 3, 4).reshape(bsz, seq, MIXER_WIDTH)
    y = jax.nn.gelu(y + d_skip.astype(f32) * u32).astype(u.dtype)
    return y * jax.nn.sigmoid(y @ w_glu + b_glu)


def _causal_depthwise_conv(x, w):
    k = w.shape[0]
    return lax.conv_general_dilated(
        x, w[:, None, :].astype(x.dtype), window_strides=(1,), padding=[(k - 1, 0)],
        dimension_numbers=('NWC', 'WIO', 'NWC'), feature_group_count=x.shape[-1])


def _gdn_mixer(proj, conv_w, a_log, dt_bias, o_norm):
    bsz, seq, _ = proj.shape
    f32 = jnp.float32
    wd, nh, hd, cl = MIXER_WIDTH, GDN_HEADS, HEAD_DIM, GDN_CHUNK
    nc = seq // cl
    qkv = jax.nn.silu(_causal_depthwise_conv(proj[..., :3 * wd], conv_w)).astype(f32)
    gate = proj[..., 3 * wd:4 * wd].astype(f32).reshape(bsz, seq, nh, hd)
    a_in = proj[..., 4 * wd:4 * wd + nh].astype(f32)
    b_in = proj[..., 4 * wd + nh:4 * wd + 2 * nh].astype(f32)
    q = _l2norm(qkv[..., :wd].reshape(bsz, seq, nh, hd)) * hd ** -0.5
    k = _l2norm(qkv[..., wd:2 * wd].reshape(bsz, seq, nh, hd))
    v = qkv[..., 2 * wd:].reshape(bsz, seq, nh, hd)
    beta = jax.nn.sigmoid(b_in)
    g = -jnp.exp(a_log.astype(f32)) * jax.nn.softplus(a_in + dt_bias.astype(f32))

    def chunks(t):
        return t.reshape(bsz, nc, cl, nh, -1).transpose(0, 3, 1, 2, 4)

    q, k, v = chunks(q), chunks(k), chunks(v)
    beta = chunks(beta[..., None])
    gc = jnp.cumsum(chunks(g[..., None])[..., 0], axis=-1)
    idx = jnp.arange(cl)
    lower = idx[:, None] >= idx[None, :]
    strict = idx[:, None] > idx[None, :]
    decay = jnp.exp(jnp.where(lower, gc[..., :, None] - gc[..., None, :], -jnp.inf))
    kb, vb = k * beta, v * beta
    lmat = jnp.where(strict, jnp.einsum('bhncd,bhnsd->bhncs', kb, k) * decay, 0.0)
    rhs = jnp.concatenate([vb, kb * jnp.exp(gc)[..., None]], axis=-1)
    sol = lax.linalg.triangular_solve(lmat + jnp.eye(cl, dtype=f32), rhs, left_side=True,
                                      lower=True, unit_diagonal=True)
    u_c, w_c = sol[..., :hd], sol[..., hd:]
    attn_in = jnp.where(lower, jnp.einsum('bhncd,bhnsd->bhncs', q, k) * decay, 0.0)
    q_dec = q * jnp.exp(gc)[..., None]
    k_dec = k * jnp.exp(gc[..., -1:] - gc)[..., None]
    g_last = jnp.exp(gc[..., -1])
    xs = tuple(jnp.moveaxis(t, 2, 0) for t in (u_c, w_c, attn_in, q_dec, k_dec, g_last))

    def step(state, inp):
        u_i, w_i, a_i, qd_i, kd_i, gl_i = inp
        v_new = u_i - jnp.einsum('bhck,bhkv->bhcv', w_i, state)
        out = jnp.einsum('bhck,bhkv->bhcv', qd_i, state) + jnp.einsum('bhcs,bhsv->bhcv', a_i, v_new)
        state = state * gl_i[..., None, None] + jnp.einsum('bhck,bhcv->bhkv', kd_i, v_new)
        return state, out

    _, o = lax.scan(step, jnp.zeros((bsz, nh, hd, hd), f32), xs)
    o = o.transpose(1, 0, 3, 2, 4).reshape(bsz, seq, nh, hd)
    o = _rmsnorm(o, o_norm) * jax.nn.silu(gate)
    return o.reshape(bsz, seq, wd).astype(proj.dtype)


def _fox_mixer(proj, b_f):
    bsz, seq, _ = proj.shape
    f32 = jnp.float32
    wd, nh, hd = MIXER_WIDTH, FOX_HEADS, HEAD_DIM

    def heads(t):
        return t.reshape(bsz, seq, nh, hd).transpose(0, 2, 1, 3)

    q, k, v = heads(proj[..., :wd]), heads(proj[..., wd:2 * wd]), heads(proj[..., 2 * wd:3 * wd])
    f_logit = proj[..., 3 * wd:3 * wd + nh].astype(f32) + b_f.astype(f32)
    cum_f = jnp.cumsum(jax.nn.log_sigmoid(f_logit), axis=1).transpose(0, 2, 1)
    scale = hd ** -0.5
    q_off = jnp.arange(FOX_BLOCK)
    outs = []
    for blk in range(seq // FOX_BLOCK):
        q0, q1 = blk * FOX_BLOCK, (blk + 1) * FOX_BLOCK
        logits = (jnp.einsum('bhqd,bhkd->bhqk', q[:, :, q0:q1], k[:, :, :q1]).astype(f32) * scale
                  + cum_f[:, :, q0:q1, None] - cum_f[:, :, None, :q1])
        causal = (q0 + q_off)[:, None] >= jnp.arange(q1)[None, :]
        p = jax.nn.softmax(jnp.where(causal, logits, -jnp.inf), axis=-1).astype(v.dtype)
        outs.append(jnp.einsum('bhqk,bhkd->bhqd', p, v[:, :, :q1]))
    o = jnp.concatenate(outs, axis=2)
    return o.transpose(0, 2, 1, 3).reshape(bsz, seq, wd)


def _memory_attention(q_proj, mem_k, mem_v):
    bsz, seq, _ = q_proj.shape
    q = q_proj.reshape(bsz, seq, MEM_HEADS, HEAD_DIM)
    logits = jnp.einsum('bshd,bmhd->bhsm', q, mem_k).astype(jnp.float32) * HEAD_DIM ** -0.5
    p = jax.nn.softmax(logits, axis=-1).astype(mem_v.dtype)
    return jnp.einsum('bhsm,bmhd->bshd', p, mem_v).reshape(bsz, seq, MEM_WIDTH)


def setup_inputs(seed: int = 0) -> dict:
    key = jax.random.key(seed)
    keys = iter(jax.random.split(key, 32))
    f32 = jnp.float32

    def normal(shape, scale):
        return scale * jax.random.normal(next(keys), shape, f32)

    def gain(shape):
        return 1.0 + normal(shape, 0.05)

    def uniform(shape, lo, hi):
        return jax.random.uniform(next(keys), shape, f32, lo, hi)

    x = normal((BATCH, SEQ, D_MODEL), 1.0)
    mem = normal((BATCH, MEM_LEN, D_MODEL), 1.0)
    mem_norm = gain((D_MODEL,))
    w_mem_kv = normal((D_MODEL, 2 * MEM_WIDTH), D_MODEL ** -0.5)
    norm1 = gain((DEPTH, D_MODEL))
    w_out = normal((DEPTH, MIX_WIDTH, D_MODEL), MIX_WIDTH ** -0.5)
    norm2 = gain((DEPTH, D_MODEL))
    w_up = normal((DEPTH, D_MODEL, D_FF), D_MODEL ** -0.5)
    w_down = normal((DEPTH, D_FF, D_MODEL), D_FF ** -0.5)
    norm_f = gain((D_MODEL,))
    s5_w_in = normal((N_S5, D_MODEL, S5_IN), D_MODEL ** -0.5)
    s5_lam_re = -0.5 * jnp.exp(normal((N_S5, S5_GROUPS, S5_STATE), 0.05))
    s5_lam_im = math.pi * jnp.arange(S5_STATE, dtype=f32) + normal((N_S5, S5_GROUPS, S5_STATE), 0.01)
    s5_log_dt = uniform((N_S5, S5_GROUPS), math.log(1e-3), math.log(1e-1))
    s5_b_re = normal((N_S5, S5_GROUPS, S5_STATE, S5_GROUP), (2 * S5_GROUP) ** -0.5)
    s5_b_im = normal((N_S5, S5_GROUPS, S5_STATE, S5_GROUP), (2 * S5_GROUP) ** -0.5)
    s5_c_re = normal((N_S5, S5_GROUPS, S5_GROUP, S5_STATE), S5_STATE ** -0.5)
    s5_c_im = normal((N_S5, S5_GROUPS, S5_GROUP, S5_STATE), S5_STATE ** -0.5)
    s5_d_skip = normal((N_S5, MIXER_WIDTH), 1.0)
    s5_w_glu = normal((N_S5, MIXER_WIDTH, MIXER_WIDTH), MIXER_WIDTH ** -0.5)
    s5_b_glu = normal((N_S5, MIXER_WIDTH), 0.01)
    gdn_w_in = normal((N_GDN, D_MODEL, GDN_IN), D_MODEL ** -0.5)
    gdn_conv_w = normal((N_GDN, GDN_CONV, 3 * MIXER_WIDTH), GDN_CONV ** -0.5)
    gdn_a_log = jnp.log(uniform((N_GDN, GDN_HEADS), 1.0, 16.0))
    dt = jnp.exp(uniform((N_GDN, GDN_HEADS), math.log(1e-3), math.log(1e-1)))
    gdn_dt_bias = dt + jnp.log(-jnp.expm1(-dt))
    gdn_o_norm = gain((N_GDN, HEAD_DIM))
    fox_w_in = normal((N_FOX, D_MODEL, FOX_IN), D_MODEL ** -0.5)
    fox_b_f = uniform((N_FOX, FOX_HEADS), 1.0, 6.0)
    return {'x': x, 'mem': mem, 'mem_norm': mem_norm, 'w_mem_kv': w_mem_kv,
            'norm1': norm1, 'w_out': w_out, 'norm2': norm2, 'w_up': w_up, 'w_down': w_down,
            'norm_f': norm_f,
            's5_w_in': s5_w_in, 's5_lam_re': s5_lam_re, 's5_lam_im': s5_lam_im,
            's5_log_dt': s5_log_dt, 's5_b_re': s5_b_re, 's5_b_im': s5_b_im,
            's5_c_re': s5_c_re, 's5_c_im': s5_c_im, 's5_d_skip': s5_d_skip,
            's5_w_glu': s5_w_glu, 's5_b_glu': s5_b_glu,
            'gdn_w_in': gdn_w_in, 'gdn_conv_w': gdn_conv_w, 'gdn_a_log': gdn_a_log,
            'gdn_dt_bias': gdn_dt_bias, 'gdn_o_norm': gdn_o_norm,
            'fox_w_in': fox_w_in, 'fox_b_f': fox_b_f}


def reference(x, mem, mem_norm, w_mem_kv, norm1, w_out, norm2, w_up, w_down, norm_f,
              s5_w_in, s5_lam_re, s5_lam_im, s5_log_dt, s5_b_re, s5_b_im, s5_c_re, s5_c_im,
              s5_d_skip, s5_w_glu, s5_b_glu,
              gdn_w_in, gdn_conv_w, gdn_a_log, gdn_dt_bias, gdn_o_norm,
              fox_w_in, fox_b_f):
    bsz = x.shape[0]
    mkv = _rmsnorm(mem, mem_norm) @ w_mem_kv
    mem_k = mkv[..., :MEM_WIDTH].reshape(bsz, MEM_LEN, MEM_HEADS, HEAD_DIM)
    mem_v = mkv[..., MEM_WIDTH:].reshape(bsz, MEM_LEN, MEM_HEADS, HEAD_DIM)
    h = x
    for i in range(DEPTH):
        kind, j = i % N_MIXERS, i // N_MIXERS
        a = _rmsnorm(h, norm1[i])
        if kind == 0:
            proj = a @ s5_w_in[j]
            mix = _s5_mixer(proj[..., :-MEM_WIDTH], s5_lam_re[j], s5_lam_im[j], s5_log_dt[j],
                            s5_b_re[j], s5_b_im[j], s5_c_re[j], s5_c_im[j], s5_d_skip[j],
                            s5_w_glu[j], s5_b_glu[j])
        elif kind == 1:
            proj = a @ gdn_w_in[j]
            mix = _gdn_mixer(proj[..., :-MEM_WIDTH], gdn_conv_w[j], gdn_a_log[j],
                             gdn_dt_bias[j], gdn_o_norm[j])
        else:
            proj = a @ fox_w_in[j]
            mix = _fox_mixer(proj[..., :-MEM_WIDTH], fox_b_f[j])
        read = _memory_attention(proj[..., -MEM_WIDTH:], mem_k, mem_v)
        h = h + jnp.concatenate([mix, read], axis=-1) @ w_out[i]
        a = _rmsnorm(h, norm2[i])
        h = h + jnp.square(jax.nn.relu(a @ w_up[i])) @ w_down[i]
    return _rmsnorm(h, norm_f)
```

```python
import functools
import math

import jax
import jax.numpy as jnp
from jax import lax
from jax.experimental import pallas as pl
from jax.experimental.pallas import tpu as pltpu

F32 = jnp.float32
BF16 = jnp.bfloat16
EPS = 1e-6
HEAD_DIM = 128
LANES = 128
MEM_HEADS = 4
MEM_WIDTH = MEM_HEADS * HEAD_DIM
S5_GROUP = 16
S5_STATE = 64
S5_CHUNK = 64
GDN_CHUNK = 64
GDN_CONV = 4
VMEM_LIMIT_BYTES = 48 * 1024 * 1024
HI = lax.Precision.HIGHEST


def _params(*sem):
    return pltpu.CompilerParams(dimension_semantics=sem, vmem_limit_bytes=VMEM_LIMIT_BYTES)


def _dot(a, b, precision=None):
    return jnp.dot(a, b, preferred_element_type=F32, precision=precision)


def _dot_nt(a, b, precision=None):
    return lax.dot_general(a, b, (((1,), (1,)), ((), ())), preferred_element_type=F32,
                           precision=precision)


def _dot_tn(a, b, precision=None):
    return lax.dot_general(a, b, (((0,), (0,)), ((), ())), preferred_element_type=F32,
                           precision=precision)


def _softplus(x):
    return jnp.maximum(x, 0.0) + jnp.log1p(jnp.exp(-jnp.abs(x)))


def _sigmoid(x):
    return 1.0 / (1.0 + jnp.exp(-x))


def _lane_pick(x, lane):
    ids = lax.broadcasted_iota(jnp.int32, x.shape, 1)
    return jnp.sum(jnp.where(ids == lane, x, 0.0), axis=-1, keepdims=True)


def _rmsnorm_body(x_ref, g_ref, o_ref):
    x = x_ref[...].astype(F32)
    ms = jnp.mean(x * x, axis=-1, keepdims=True)
    o_ref[...] = (x * lax.rsqrt(ms + EPS) * g_ref[...]).astype(o_ref.dtype)


def _rmsnorm(x2d, gain, out_dtype=BF16, tm=512):
    m, d = x2d.shape
    tm = min(tm, m)
    return pl.pallas_call(
        _rmsnorm_body, grid=(m // tm,),
        in_specs=[pl.BlockSpec((tm, d), lambda i: (i, 0)),
                  pl.BlockSpec((1, d), lambda i: (0, 0))],
        out_specs=pl.BlockSpec((tm, d), lambda i: (i, 0)),
        out_shape=jax.ShapeDtypeStruct((m, d), out_dtype),
        compiler_params=_params("parallel"), name="rmsnorm",
    )(x2d, gain.reshape(1, d).astype(F32))


def _mm_body(*refs, nk, act, has_res):
    a_ref, w_ref = refs[0], refs[1]
    pos = 2
    z_ref = b_ref = res_ref = None
    if act == "glu":
        z_ref, b_ref = refs[pos], refs[pos + 1]
        pos += 2
    if has_res:
        res_ref = refs[pos]
        pos += 1
    o_ref = refs[pos]
    acc_ref = refs[pos + 1] if nk > 1 else None

    def epilogue(y):
        if act == "relu2":
            y = jnp.square(jnp.maximum(y, 0.0))
        elif act == "glu":
            y = z_ref[...].astype(F32) * _sigmoid(y + b_ref[...])
        if has_res:
            y = y + res_ref[...]
        o_ref[...] = y.astype(o_ref.dtype)

    part = _dot(a_ref[...], w_ref[...])
    if nk == 1:
        epilogue(part)
        return
    k = pl.program_id(2)

    @pl.when(k == 0)
    def _():
        acc_ref[...] = part

    @pl.when(k > 0)
    def _():
        acc_ref[...] += part

    @pl.when(k == nk - 1)
    def _():
        epilogue(acc_ref[...])


def _matmul(a, w, *, out_dtype, act=None, z=None, bias=None, res=None, tm=1024, tn=1024, tk=2048):
    m, kdim = a.shape
    n = w.shape[1]
    tm, tn, tk = min(tm, m), min(tn, n), min(tk, kdim)
    assert m % tm == 0 and n % tn == 0 and kdim % tk == 0, (a.shape, w.shape, tm, tn, tk)
    nk = kdim // tk
    in_specs = [pl.BlockSpec((tm, tk), lambda i, j, k: (i, k)),
                pl.BlockSpec((tk, tn), lambda i, j, k: (k, j))]
    args = [a, w]
    if act == "glu":
        in_specs += [pl.BlockSpec((tm, tn), lambda i, j, k: (i, j)),
                     pl.BlockSpec((1, tn), lambda i, j, k: (0, j))]
        args += [z, bias.reshape(1, n).astype(F32)]
    if res is not None:
        in_specs.append(pl.BlockSpec((tm, tn), lambda i, j, k: (i, j)))
        args.append(res)
    scratch = [pltpu.VMEM((tm, tn), F32)] if nk > 1 else []
    return pl.pallas_call(
        functools.partial(_mm_body, nk=nk, act=act, has_res=res is not None),
        grid=(m // tm, n // tn, nk),
        in_specs=in_specs,
        out_specs=pl.BlockSpec((tm, tn), lambda i, j, k: (i, j)),
        out_shape=jax.ShapeDtypeStruct((m, n), out_dtype),
        scratch_shapes=scratch,
        compiler_params=_params("parallel", "parallel", "arbitrary"), name="matmul",
    )(*args)


def _mem_attn_body(q_ref, k_ref, v_ref, o_ref):
    scale = HEAD_DIM ** -0.5
    for h in range(MEM_HEADS):
        sl = slice(h * HEAD_DIM, (h + 1) * HEAD_DIM)
        logits = _dot_nt(q_ref[0, :, sl], k_ref[0, :, sl]) * scale
        mx = jnp.max(logits, axis=-1, keepdims=True)
        e = jnp.exp(logits - mx)
        p = e / jnp.sum(e, axis=-1, keepdims=True)
        o_ref[0, :, sl] = _dot(p.astype(BF16), v_ref[0, :, sl]).astype(o_ref.dtype)


def _mem_attention(q, mem_k, mem_v, ts=1024):
    bsz, seq, _ = q.shape
    mlen = mem_k.shape[1]
    ts = min(ts, seq)
    return pl.pallas_call(
        _mem_attn_body, grid=(bsz, seq // ts),
        in_specs=[pl.BlockSpec((1, ts, MEM_WIDTH), lambda b, i: (b, i, 0)),
                  pl.BlockSpec((1, mlen, MEM_WIDTH), lambda b, i: (b, 0, 0)),
                  pl.BlockSpec((1, mlen, MEM_WIDTH), lambda b, i: (b, 0, 0))],
        out_specs=pl.BlockSpec((1, ts, MEM_WIDTH), lambda b, i: (b, i, 0)),
        out_shape=jax.ShapeDtypeStruct((bsz, seq, MEM_WIDTH), BF16),
        compiler_params=_params("parallel", "parallel"), name="mem_attention",
    )(q, mem_k, mem_v)


def _s5_build_body(lre_ref, lim_ref, ldt_ref, cre_ref, cim_ref, btr_ref, bti_ref,
                   tt_ref, wbr_ref, wbi_ref, car_ref, cai_ref, al_ref, r_ref, *, chunk):
    grp, pst = S5_GROUP, S5_STATE
    l_re, l_im = lre_ref[0], lim_ref[0]
    dt = jnp.exp(ldt_ref[0])
    x_re, x_im = l_re * dt, l_im * dt
    a_mag = jnp.exp(x_re)
    a_re, a_im = a_mag * jnp.cos(x_im), a_mag * jnp.sin(x_im)
    den = l_re * l_re + l_im * l_im
    z_re = ((a_re - 1.0) * l_re + a_im * l_im) / den
    z_im = (a_im * l_re - (a_re - 1.0) * l_im) / den
    bt_re, bt_im = btr_ref[0], bti_ref[0]
    bb_re = z_re * bt_re - z_im * bt_im
    bb_im = z_re * bt_im + z_im * bt_re
    c_re, c_im = cre_ref[0], cim_ref[0]

    def powers(offset):
        t = (lax.broadcasted_iota(jnp.int32, (chunk, 1, pst), 0) + offset).astype(F32)
        mag = jnp.exp(x_re[None] * t)
        ang = x_im[None] * t
        return mag * jnp.cos(ang), mag * jnp.sin(ang)

    def times(m_re, m_im, p_re, p_im):
        o_re = m_re[None] * p_re - m_im[None] * p_im
        o_im = m_re[None] * p_im + m_im[None] * p_re
        return o_re.reshape(chunk * grp, pst), o_im.reshape(chunk * grp, pst)

    p0_re, p0_im = powers(0)
    p1_re, p1_im = powers(1)
    ca_re, ca_im = times(c_re, c_im, p1_re, p1_im)
    car_ref[0] = ca_re.astype(car_ref.dtype)
    cai_ref[0] = ca_im.astype(cai_ref.dtype)
    rev = chunk - 1 - lax.broadcasted_iota(jnp.int32, (chunk, 1, pst), 0)
    trev = rev.astype(F32)
    mag = jnp.exp(x_re[None] * trev)
    ang = x_im[None] * trev
    wb_re, wb_im = times(bb_re, bb_im, mag * jnp.cos(ang), mag * jnp.sin(ang))
    wbr_ref[0] = wb_re.astype(wbr_ref.dtype)
    wbi_ref[0] = wb_im.astype(wbi_ref.dtype)
    lf = float(chunk)
    magl = jnp.exp(x_re * lf)
    al_ref[0] = jnp.concatenate([magl * jnp.cos(x_im * lf), magl * jnp.sin(x_im * lf)], axis=0)
    ce_re, ce_im = times(c_re, c_im, p0_re, p0_im)
    r_ref[...] = _dot_nt(bb_re, ce_re, HI) - _dot_nt(bb_im, ce_im, HI)
    width = chunk * grp
    lane = lax.broadcasted_iota(jnp.int32, (grp, width), 1)
    for j in range(chunk):
        r = r_ref[...]
        if j:
            r = jnp.where(lane >= j * grp, pltpu.roll(r, j * grp, axis=1), 0.0)
        tt_ref[0, j * grp:(j + 1) * grp, :] = r.astype(tt_ref.dtype)


def _s5_build(lam_re, lam_im, log_dt, b_re, b_im, c_re, c_im, chunk):
    ng, pst = lam_re.shape
    grp = S5_GROUP
    width = chunk * grp
    row = lambda x: x.astype(F32).reshape(ng, 1, pst)
    ldt = jnp.broadcast_to(log_dt.astype(F32)[:, None, None], (ng, 1, pst))
    args = (row(lam_re), row(lam_im), ldt, c_re.astype(F32), c_im.astype(F32),
            jnp.swapaxes(b_re.astype(F32), 1, 2), jnp.swapaxes(b_im.astype(F32), 1, 2))
    spec_row = pl.BlockSpec((1, 1, pst), lambda g: (g, 0, 0))
    spec_gp = pl.BlockSpec((1, grp, pst), lambda g: (g, 0, 0))
    spec_w = pl.BlockSpec((1, width, pst), lambda g: (g, 0, 0))
    return pl.pallas_call(
        functools.partial(_s5_build_body, chunk=chunk), grid=(ng,),
        in_specs=[spec_row, spec_row, spec_row, spec_gp, spec_gp, spec_gp, spec_gp],
        out_specs=[pl.BlockSpec((1, width, width), lambda g: (g, 0, 0)),
                   spec_w, spec_w, spec_w, spec_w,
                   pl.BlockSpec((1, 2, pst), lambda g: (g, 0, 0))],
        out_shape=[jax.ShapeDtypeStruct((ng, width, width), BF16),
                   jax.ShapeDtypeStruct((ng, width, pst), BF16),
                   jax.ShapeDtypeStruct((ng, width, pst), BF16),
                   jax.ShapeDtypeStruct((ng, width, pst), BF16),
                   jax.ShapeDtypeStruct((ng, width, pst), BF16),
                   jax.ShapeDtypeStruct((ng, 2, pst), F32)],
        scratch_shapes=[pltpu.VMEM((grp, width), F32)],
        compiler_params=_params("parallel"), name="s5_build",
    )(*args)


def _s5_scan_body(u_ref, tt_ref, wbr_ref, wbi_ref, car_ref, cai_ref, al_ref, y_ref,
                  sre_ref, sim_ref, *, bsz, nc):
    u = u_ref[0]
    sre_ref[...] = _dot(u, wbr_ref[0])
    sim_ref[...] = _dot(u, wbi_ref[0])
    a_re, a_im = al_ref[0, 0:1, :], al_ref[0, 1:2, :]

    def step(c, carry):
        out = []
        for b in range(bsz):
            h_re, h_im = carry[b]
            r = b * nc + c
            loc_re, loc_im = sre_ref[pl.ds(r, 1), :], sim_ref[pl.ds(r, 1), :]
            sre_ref[pl.ds(r, 1), :] = h_re
            sim_ref[pl.ds(r, 1), :] = h_im
            out.append((a_re * h_re - a_im * h_im + loc_re, a_re * h_im + a_im * h_re + loc_im))
        return tuple(out)

    zero = jnp.zeros((1, S5_STATE), F32)
    lax.fori_loop(0, nc, step, tuple((zero, zero) for _ in range(bsz)))
    y = _dot(u, tt_ref[0])
    y = y + _dot_nt(sre_ref[...].astype(BF16), car_ref[0])
    y = y - _dot_nt(sim_ref[...].astype(BF16), cai_ref[0])
    y_ref[0] = y


def _s5_scan(u_g, tables, bsz):
    tt, wbr, wbi, car, cai, al = tables
    ng, rows, width = u_g.shape
    pst = S5_STATE
    nc = rows // bsz
    spec_w = pl.BlockSpec((1, width, pst), lambda g: (g, 0, 0))
    return pl.pallas_call(
        functools.partial(_s5_scan_body, bsz=bsz, nc=nc), grid=(ng,),
        in_specs=[pl.BlockSpec((1, rows, width), lambda g: (g, 0, 0)),
                  pl.BlockSpec((1, width, width), lambda g: (g, 0, 0)),
                  spec_w, spec_w, spec_w, spec_w,
                  pl.BlockSpec((1, 2, pst), lambda g: (g, 0, 0))],
        out_specs=pl.BlockSpec((1, rows, width), lambda g: (g, 0, 0)),
        out_shape=jax.ShapeDtypeStruct((ng, rows, width), F32),
        scratch_shapes=[pltpu.VMEM((rows, pst), F32), pltpu.VMEM((rows, pst), F32)],
        compiler_params=_params("parallel"), name="s5_scan",
    )(u_g, tt, wbr, wbi, car, cai, al)


def _s5_act_body(y_ref, u_ref, d_ref, o_ref):
    o_ref[...] = jax.nn.gelu(y_ref[...] + d_ref[...] * u_ref[...]).astype(o_ref.dtype)


def _s5_act(y, u, d_skip, tm=512):
    m, w = y.shape
    tm = min(tm, m)
    spec = pl.BlockSpec((tm, w), lambda i: (i, 0))
    return pl.pallas_call(
        _s5_act_body, grid=(m // tm,),
        in_specs=[spec, spec, pl.BlockSpec((1, w), lambda i: (0, 0))],
        out_specs=spec, out_shape=jax.ShapeDtypeStruct((m, w), BF16),
        compiler_params=_params("parallel"), name="s5_act",
    )(y, u, d_skip.reshape(1, w).astype(F32))


def _s5_mixer(u, lam_re, lam_im, log_dt, b_re, b_im, c_re, c_im, d_skip, w_glu, b_glu, bsz):
    m, width = u.shape
    seq = m // bsz
    ng = width // S5_GROUP
    chunk = min(S5_CHUNK, seq)
    nc = seq // chunk
    tables = _s5_build(lam_re, lam_im, log_dt, b_re, b_im, c_re, c_im, chunk)
    u_g = u.reshape(bsz * nc, chunk, ng, S5_GROUP).transpose(2, 0, 1, 3)
    u_g = u_g.reshape(ng, bsz * nc, chunk * S5_GROUP).astype(BF16)
    y_g = _s5_scan(u_g, tables, bsz)
    y = y_g.reshape(ng, bsz * nc, chunk, S5_GROUP).transpose(1, 2, 0, 3).reshape(m, width)
    z = _s5_act(y, u, d_skip)
    return _matmul(z, w_glu.astype(BF16), out_dtype=BF16, act="glu", z=z, bias=b_glu,
                   tn=min(width, 768), tk=width)


def _gdn_conv_body(x_ref, halo_ref, w_ref, o_ref, buf_ref, *, ts, nh):
    i = pl.program_id(1)
    j = pl.program_id(2)
    halo = halo_ref[0]
    buf_ref[0:8, :] = jnp.where(i == 0, jnp.zeros_like(halo), halo)
    buf_ref[8:8 + ts, :] = x_ref[0]
    acc = None
    for tap in range(GDN_CONV):
        off = 8 - (GDN_CONV - 1) + tap
        term = buf_ref[off:off + ts, :] * w_ref[tap:tap + 1, :]
        acc = term if acc is None else acc + term
    y = acc * _sigmoid(acc)
    qscale = jnp.where(j == 0, HEAD_DIM ** -0.5, 1.0).astype(F32)
    for h in range(nh):
        sl = slice(h * HEAD_DIM, (h + 1) * HEAD_DIM)
        yh = y[:, sl]
        nrm = yh * lax.rsqrt(jnp.sum(yh * yh, axis=-1, keepdims=True) + EPS) * qscale
        o_ref[0, :, sl] = jnp.where(j == 2, yh, nrm)


def _gdn_conv(x, conv_w, ts=256):
    bsz, seq, w4 = x.shape
    wd = w4 // 4
    w3 = 3 * wd
    ts = min(ts, seq)
    per8 = ts // 8
    return pl.pallas_call(
        functools.partial(_gdn_conv_body, ts=ts, nh=wd // HEAD_DIM),
        grid=(bsz, seq // ts, 3),
        in_specs=[pl.BlockSpec((1, ts, wd), lambda b, i, j: (b, i, j)),
                  pl.BlockSpec((1, 8, wd), lambda b, i, j: (b, jnp.maximum(i * per8 - 1, 0), j)),
                  pl.BlockSpec((GDN_CONV, wd), lambda b, i, j: (0, j))],
        out_specs=pl.BlockSpec((1, ts, wd), lambda b, i, j: (b, i, j)),
        out_shape=jax.ShapeDtypeStruct((bsz, seq, w3), F32),
        scratch_shapes=[pltpu.VMEM((ts + 8, wd), F32)],
        compiler_params=_params("parallel", "parallel", "parallel"), name="gdn_conv",
    )(x, x, conv_w.astype(F32))


_G_GC, _G_EGC, _G_EDEC, _G_BETA, _G_GLAST = 0, 16, 32, 48, 64


def _gdn_gates_body(x_ref, alog_ref, dtb_ref, o_ref, *, ts):
    x = x_ref[...]
    g = -jnp.exp(alog_ref[...]) * _softplus(x + dtb_ref[...])
    r = lax.broadcasted_iota(jnp.int32, (ts, ts), 0)
    c = lax.broadcasted_iota(jnp.int32, (ts, ts), 1)
    same = (r // GDN_CHUNK) == (c // GDN_CHUNK)
    gc = _dot(jnp.where(same & (r >= c), 1.0, 0.0).astype(F32), g, HI)
    gl = _dot(jnp.where(same, 1.0, 0.0).astype(F32), g, HI)
    lane = lax.broadcasted_iota(jnp.int32, x.shape, 1)
    out = jnp.where(lane < _G_EGC, gc,
          jnp.where(lane < _G_EDEC, jnp.exp(gc),
          jnp.where(lane < _G_BETA, jnp.exp(gl - gc),
          jnp.where(lane < _G_GLAST, _sigmoid(x), jnp.exp(gl)))))
    o_ref[...] = out


def _gdn_gates(ab, a_log, dt_bias, ts=512):
    m = ab.shape[0]
    ts = min(ts, m)
    nh = a_log.shape[0]

    def lanes(p):
        row = jnp.zeros((LANES,), F32)
        for off in (_G_GC, _G_EGC, _G_EDEC, _G_GLAST):
            row = row.at[off:off + nh].set(p.astype(F32))
        return row.reshape(1, LANES)

    spec = pl.BlockSpec((ts, LANES), lambda i: (i, 0))
    prm = pl.BlockSpec((1, LANES), lambda i: (0, 0))
    return pl.pallas_call(
        functools.partial(_gdn_gates_body, ts=ts), grid=(m // ts,),
        in_specs=[spec, prm, prm], out_specs=spec,
        out_shape=jax.ShapeDtypeStruct((m, LANES), F32),
        compiler_params=_params("parallel"), name="gdn_gates",
    )(ab, lanes(a_log), lanes(dt_bias))


def _gdn_intra_body(q_ref, k_ref, v_ref, gp_ref, gct_ref, u_ref, w_ref, qd_ref, kd_ref, at_ref,
                    *, nb):
    cl = GDN_CHUNK
    h = pl.program_id(1)
    gp = gp_ref[0]
    gc_col = _lane_pick(gp, h + _G_GC)
    egc_col = _lane_pick(gp, h + _G_EGC)
    edec_col = _lane_pick(gp, h + _G_EDEC)
    beta_col = _lane_pick(gp, h + _G_BETA)
    r = lax.broadcasted_iota(jnp.int32, (cl, cl), 0)
    c = lax.broadcasted_iota(jnp.int32, (cl, cl), 1)
    for n in range(nb):
        rows = slice(n * cl, (n + 1) * cl)
        q, k, v = q_ref[0, rows, :], k_ref[0, rows, :], v_ref[0, rows, :]
        beta, egc = beta_col[rows], egc_col[rows]
        gc_row = gct_ref[0, 0, :, rows]
        decay = jnp.exp(jnp.where(r >= c, gc_col[rows] - gc_row, -jnp.inf))
        kb, vb = k * beta, v * beta
        p = -jnp.where(r > c, _dot_nt(kb.astype(BF16), k.astype(BF16)) * decay, 0.0)
        sol = jnp.concatenate([vb, kb * egc], axis=-1)
        steps = int(math.log2(cl))
        for s in range(steps):
            sol = sol + _dot(p, sol, HI)
            if s + 1 < steps:
                p = _dot(p, p, HI)
        u_ref[0, rows, :] = sol[:, :HEAD_DIM]
        w_ref[0, rows, :] = sol[:, HEAD_DIM:]
        qd_ref[0, rows, :] = q * egc
        kd_ref[0, rows, :] = k * edec_col[rows]
        at_ref[0, 0, rows, :] = jnp.where(r >= c, _dot_nt(q.astype(BF16), k.astype(BF16)) * decay, 0.0)


def _gdn_intra(qkv, gates, gct, nb=4):
    bsz, seq, w3 = qkv.shape
    wd = w3 // 3
    nh = wd // HEAD_DIM
    cl = GDN_CHUNK
    nb = min(nb, seq // cl)
    tt = nb * cl
    col = lambda off: pl.BlockSpec((1, tt, HEAD_DIM), lambda b, h, i: (b, i, h + off))
    out_sd = jax.ShapeDtypeStruct((bsz, seq, wd), F32)
    return pl.pallas_call(
        functools.partial(_gdn_intra_body, nb=nb), grid=(bsz, nh, seq // tt),
        in_specs=[col(0), col(nh), col(2 * nh),
                  pl.BlockSpec((1, tt, LANES), lambda b, h, i: (b, i, 0)),
                  pl.BlockSpec((1, 1, 1, tt), lambda b, h, i: (b, h, 0, i))],
        out_specs=[col(0), col(0), col(0), col(0),
                   pl.BlockSpec((1, 1, tt, cl), lambda b, h, i: (b, h, i, 0))],
        out_shape=[out_sd, out_sd, out_sd, out_sd,
                   jax.ShapeDtypeStruct((bsz, nh, seq, cl), F32)],
        compiler_params=_params("parallel", "parallel", "parallel"), name="gdn_intra",
    )(qkv, qkv, qkv, gates, gct)


def _gdn_scan_body(gl_ref, u_ref, w_ref, qd_ref, kd_ref, at_ref, gate_ref, on_ref, o_ref,
                   state_ref, *, nb, nh):
    cl = GDN_CHUNK
    b = pl.program_id(0)
    i = pl.program_id(1)

    @pl.when(i == 0)
    def _():
        state_ref[...] = jnp.zeros_like(state_ref)

    for n in range(nb):
        rows = slice(n * cl, (n + 1) * cl)
        for h in range(nh):
            sl = slice(h * HEAD_DIM, (h + 1) * HEAD_DIM)
            state = state_ref[h]
            sb = state.astype(BF16)
            v_new = u_ref[0, rows, sl] - _dot(w_ref[0, rows, sl].astype(BF16), sb)
            vb = v_new.astype(BF16)
            out = _dot(qd_ref[0, rows, sl].astype(BF16), sb) + _dot(at_ref[0, h, rows, :].astype(BF16), vb)
            g_last = gl_ref[b, i * nb + n, h]
            state_ref[h] = state * g_last + _dot_tn(kd_ref[0, rows, sl].astype(BF16), vb)
            gate = gate_ref[0, rows, sl]
            nrm = out * lax.rsqrt(jnp.mean(out * out, axis=-1, keepdims=True) + EPS) * on_ref[...]
            o_ref[0, rows, sl] = (nrm * (gate * _sigmoid(gate))).astype(o_ref.dtype)


def _gdn_scan(glast, u_c, w_c, q_dec, k_dec, attn, main, o_norm, nb=2):
    bsz, seq, wd = u_c.shape
    nh = wd // HEAD_DIM
    cl = GDN_CHUNK
    nb = min(nb, seq // cl)
    tt = nb * cl
    tok = pl.BlockSpec((1, tt, wd), lambda b, i: (b, i, 0))
    return pl.pallas_call(
        functools.partial(_gdn_scan_body, nb=nb, nh=nh), grid=(bsz, seq // tt),
        in_specs=[pl.BlockSpec(memory_space=pltpu.SMEM), tok, tok, tok, tok,
                  pl.BlockSpec((1, nh, tt, cl), lambda b, i: (b, 0, i, 0)),
                  pl.BlockSpec((1, tt, wd), lambda b, i: (b, i, 3)),
                  pl.BlockSpec((1, HEAD_DIM), lambda b, i: (0, 0))],
        out_specs=tok,
        out_shape=jax.ShapeDtypeStruct((bsz, seq, wd), BF16),
        scratch_shapes=[pltpu.VMEM((nh, HEAD_DIM, HEAD_DIM), F32)],
        compiler_params=_params("parallel", "arbitrary"), name="gdn_scan",
    )(glast, u_c, w_c, q_dec, k_dec, attn, main, o_norm.reshape(1, HEAD_DIM).astype(F32))


def _gdn_mixer(main, ab, conv_w, a_log, dt_bias, o_norm):
    bsz, seq, w4 = main.shape
    nh = w4 // 4 // HEAD_DIM
    qkv = _gdn_conv(main, conv_w)
    gates = _gdn_gates(ab, a_log, dt_bias).reshape(bsz, seq, LANES)
    gct = gates[:, :, _G_GC:_G_GC + nh].transpose(0, 2, 1).reshape(bsz, nh, 1, seq)
    glast = gates[:, GDN_CHUNK - 1::GDN_CHUNK, _G_GLAST:_G_GLAST + nh]
    u_c, w_c, q_dec, k_dec, attn = _gdn_intra(qkv, gates, gct)
    return _gdn_scan(glast, u_c, w_c, q_dec, k_dec, attn, main, o_norm)


def _fox_cumf_body(x_ref, bf_ref, o_ref, carry_ref, *, ts):
    @pl.when(pl.program_id(1) == 0)
    def _():
        carry_ref[...] = jnp.zeros_like(carry_ref)

    ls = -_softplus(-(x_ref[0] + bf_ref[...]))
    r = lax.broadcasted_iota(jnp.int32, (ts, ts), 0)
    c = lax.broadcasted_iota(jnp.int32, (ts, ts), 1)
    cum = _dot(jnp.where(r >= c, 1.0, 0.0).astype(F32), ls, HI) + carry_ref[...]
    o_ref[0] = cum
    carry_ref[...] = cum[ts - 1:ts, :]


def _fox_cumf(f_logit, b_f, ts=256):
    bsz, seq, _ = f_logit.shape
    ts = min(ts, seq)
    bf = jnp.zeros((LANES,), F32).at[:b_f.shape[0]].set(b_f.astype(F32)).reshape(1, LANES)
    spec = pl.BlockSpec((1, ts, LANES), lambda b, i: (b, i, 0))
    return pl.pallas_call(
        functools.partial(_fox_cumf_body, ts=ts), grid=(bsz, seq // ts),
        in_specs=[spec, pl.BlockSpec((1, LANES), lambda b, i: (0, 0))],
        out_specs=spec, out_shape=jax.ShapeDtypeStruct((bsz, seq, LANES), F32),
        scratch_shapes=[pltpu.VMEM((1, LANES), F32)],
        compiler_params=_params("parallel", "arbitrary"), name="fox_cumf",
    )(f_logit, bf)


def _fox_attn_body(q_ref, k_ref, v_ref, cf_ref, cft_ref, o_ref, m_ref, l_ref, acc_ref, cq_ref,
                   *, tq, tk):
    h = pl.program_id(1)
    qi = pl.program_id(2)
    ki = pl.program_id(3)
    scale = HEAD_DIM ** -0.5

    @pl.when(ki == 0)
    def _():
        m_ref[...] = jnp.full_like(m_ref, -jnp.inf)
        l_ref[...] = jnp.zeros_like(l_ref)
        acc_ref[...] = jnp.zeros_like(acc_ref)
        cq_ref[...] = _lane_pick(cf_ref[0], h)

    def update(masked):
        logits = _dot_nt(q_ref[0], k_ref[0]) * scale + cq_ref[...] - cft_ref[0, 0]
        if masked:
            r = lax.broadcasted_iota(jnp.int32, (tq, tk), 0)
            c = lax.broadcasted_iota(jnp.int32, (tq, tk), 1)
            logits = jnp.where(r >= c, logits, -jnp.inf)
        m_prev = m_ref[...]
        m_new = jnp.maximum(m_prev, jnp.max(logits, axis=-1, keepdims=True))
        alpha = jnp.exp(m_prev - m_new)
        p = jnp.exp(logits - m_new)
        l_ref[...] = alpha * l_ref[...] + jnp.sum(p, axis=-1, keepdims=True)
        acc_ref[...] = alpha * acc_ref[...] + _dot(p.astype(BF16), v_ref[0])
        m_ref[...] = m_new

    @pl.when(ki < qi)
    def _():
        update(False)

    @pl.when(ki == qi)
    def _():
        update(True)
        o_ref[0] = (acc_ref[...] / l_ref[...]).astype(o_ref.dtype)


def _fox_attention(qkv, cumf, cumf_t, tq=512):
    bsz, seq, w3 = qkv.shape
    wd = w3 // 3
    nh = wd // HEAD_DIM
    tq = min(tq, seq)
    tk = tq
    nq = seq // tq
    kv = lambda off: pl.BlockSpec((1, tk, HEAD_DIM),
                                  lambda b, h, qi, ki: (b, jnp.minimum(ki, qi), h + off))
    return pl.pallas_call(
        functools.partial(_fox_attn_body, tq=tq, tk=tk), grid=(bsz, nh, nq, nq),
        in_specs=[pl.BlockSpec((1, tq, HEAD_DIM), lambda b, h, qi, ki: (b, qi, h)),
                  kv(nh), kv(2 * nh),
                  pl.BlockSpec((1, tq, LANES), lambda b, h, qi, ki: (b, qi, 0)),
                  pl.BlockSpec((1, 1, 1, tk), lambda b, h, qi, ki: (b, h, 0, jnp.minimum(ki, qi)))],
        out_specs=pl.BlockSpec((1, tq, HEAD_DIM), lambda b, h, qi, ki: (b, qi, h)),
        out_shape=jax.ShapeDtypeStruct((bsz, seq, wd), BF16),
        scratch_shapes=[pltpu.VMEM((tq, 1), F32), pltpu.VMEM((tq, 1), F32),
                        pltpu.VMEM((tq, HEAD_DIM), F32), pltpu.VMEM((tq, 1), F32)],
        compiler_params=_params("parallel", "parallel", "parallel", "arbitrary"),
        name="fox_attention",
    )(qkv, qkv, qkv, cumf, cumf_t)


def _fox_mixer(qkv, f_logit, b_f):
    bsz, seq, w3 = qkv.shape
    nh = w3 // 3 // HEAD_DIM
    cumf = _fox_cumf(f_logit, b_f)
    cumf_t = cumf[:, :, :nh].transpose(0, 2, 1).reshape(bsz, nh, 1, seq)
    return _fox_attention(qkv, cumf, cumf_t)


def _pad_cols(w, groups):
    out = jnp.zeros((w.shape[0], LANES), w.dtype)
    for off in groups:
        out = out.at[:, off:off + w.shape[1]].set(w)
    return out


def kernel(x, mem, mem_norm, w_mem_kv, norm1, w_out, norm2, w_up, w_down, norm_f,
           s5_w_in, s5_lam_re, s5_lam_im, s5_log_dt, s5_b_re, s5_b_im, s5_c_re, s5_c_im,
           s5_d_skip, s5_w_glu, s5_b_glu,
           gdn_w_in, gdn_conv_w, gdn_a_log, gdn_dt_bias, gdn_o_norm,
           fox_w_in, fox_b_f):
    bsz, seq, d = x.shape
    m = bsz * seq
    depth = norm1.shape[0]
    wd = d - MEM_WIDTH
    nh = wd // HEAD_DIM
    mlen = mem.shape[1]

    mem_a = _rmsnorm(mem.reshape(bsz * mlen, d), mem_norm)
    mkv = _matmul(mem_a, w_mem_kv.astype(BF16), out_dtype=BF16, tn=512)
    mem_k = mkv[:, :MEM_WIDTH].reshape(bsz, mlen, MEM_WIDTH)
    mem_v = mkv[:, MEM_WIDTH:].reshape(bsz, mlen, MEM_WIDTH)

    h = x.reshape(m, d)
    for i in range(depth):
        kind, j = i % 3, i // 3
        a = _rmsnorm(h, norm1[i])
        if kind == 0:
            w_in = s5_w_in[j].astype(BF16)
            u = _matmul(a, w_in[:, :wd], out_dtype=F32, tn=768)
            mq = _matmul(a, w_in[:, wd:], out_dtype=BF16, tn=512)
            mix = _s5_mixer(u, s5_lam_re[j], s5_lam_im[j], s5_log_dt[j], s5_b_re[j], s5_b_im[j],
                            s5_c_re[j], s5_c_im[j], s5_d_skip[j], s5_w_glu[j], s5_b_glu[j], bsz)
        elif kind == 1:
            w_in = gdn_w_in[j]
            w_main = w_in[:, :4 * wd].astype(BF16)
            w_a = w_in[:, 4 * wd:4 * wd + nh]
            w_b = w_in[:, 4 * wd + nh:4 * wd + 2 * nh]
            w_ab = (_pad_cols(w_a, (_G_GC, _G_EGC, _G_EDEC, _G_GLAST))
                    + _pad_cols(w_b, (_G_BETA,))).astype(BF16)
            main = _matmul(a, w_main, out_dtype=F32, tn=768)
            ab = _matmul(a, w_ab, out_dtype=F32)
            mq = _matmul(a, w_in[:, -MEM_WIDTH:].astype(BF16), out_dtype=BF16, tn=512)
            main = main.reshape(bsz, seq, 4 * wd)
            mix = _gdn_mixer(main, ab, gdn_conv_w[j], gdn_a_log[j], gdn_dt_bias[j],
                             gdn_o_norm[j]).reshape(m, wd)
        else:
            w_in = fox_w_in[j]
            qkv = _matmul(a, w_in[:, :3 * wd].astype(BF16), out_dtype=BF16, tn=768)
            fl = _matmul(a, _pad_cols(w_in[:, 3 * wd:3 * wd + nh], (0,)).astype(BF16), out_dtype=F32)
            mq = _matmul(a, w_in[:, -MEM_WIDTH:].astype(BF16), out_dtype=BF16, tn=512)
            mix = _fox_mixer(qkv.reshape(bsz, seq, 3 * wd), fl.reshape(bsz, seq, LANES),
                             fox_b_f[j]).reshape(m, wd)
        read = _mem_attention(mq.reshape(bsz, seq, MEM_WIDTH), mem_k, mem_v).reshape(m, MEM_WIDTH)
        cat = jnp.concatenate([mix, read], axis=-1)
        h = _matmul(cat, w_out[i].astype(BF16), out_dtype=F32, res=h)
        a = _rmsnorm(h, norm2[i])
        up = _matmul(a, w_up[i].astype(BF16), out_dtype=BF16, act="relu2")
        h = _matmul(up, w_down[i].astype(BF16), out_dtype=F32, res=h)
    return _rmsnorm(h, norm_f, out_dtype=x.dtype).reshape(bsz, seq, d)
```

```python
import functools
import math

import jax
import jax.numpy as jnp
from jax import lax
from jax.experimental import pallas as pl
from jax.experimental.pallas import tpu as pltpu

F32 = jnp.float32
BF16 = jnp.bfloat16
EPS = 1e-6
HEAD_DIM = 128
LANES = 128
MEM_HEADS = 4
MEM_WIDTH = MEM_HEADS * HEAD_DIM
S5_GROUP = 16
S5_STATE = 64
S5_CHUNK = 64
GDN_CHUNK = 64
GDN_CONV = 4
VMEM_LIMIT_BYTES = 48 * 1024 * 1024
HI = lax.Precision.HIGHEST


def _params(*sem):
    return pltpu.CompilerParams(dimension_semantics=sem, vmem_limit_bytes=VMEM_LIMIT_BYTES)


def _dot(a, b, precision=None):
    return jnp.dot(a, b, preferred_element_type=F32, precision=precision)


def _dot_nt(a, b, precision=None):
    return lax.dot_general(a, b, (((1,), (1,)), ((), ())), preferred_element_type=F32,
                           precision=precision)


def _dot_tn(a, b, precision=None):
    return lax.dot_general(a, b, (((0,), (0,)), ((), ())), preferred_element_type=F32,
                           precision=precision)


def _softplus(x):
    return jnp.maximum(x, 0.0) + jnp.log1p(jnp.exp(-jnp.abs(x)))


def _sigmoid(x):
    return 1.0 / (1.0 + jnp.exp(-x))


def _lane_pick(x, lane):
    ids = lax.broadcasted_iota(jnp.int32, x.shape, 1)
    return jnp.sum(jnp.where(ids == lane, x, 0.0), axis=-1, keepdims=True)


def _rmsnorm_body(x_ref, g_ref, o_ref):
    x = x_ref[...].astype(F32)
    ms = jnp.mean(x * x, axis=-1, keepdims=True)
    o_ref[...] = (x * lax.rsqrt(ms + EPS) * g_ref[...]).astype(o_ref.dtype)


def _rmsnorm(x2d, gain, out_dtype=BF16, tm=512):
    m, d = x2d.shape
    tm = min(tm, m)
    return pl.pallas_call(
        _rmsnorm_body, grid=(m // tm,),
        in_specs=[pl.BlockSpec((tm, d), lambda i: (i, 0)),
                  pl.BlockSpec((1, d), lambda i: (0, 0))],
        out_specs=pl.BlockSpec((tm, d), lambda i: (i, 0)),
        out_shape=jax.ShapeDtypeStruct((m, d), out_dtype),
        compiler_params=_params("parallel"), name="rmsnorm",
    )(x2d, gain.reshape(1, d).astype(F32))


def _mm_body(*refs, nk, act, has_res):
    a_ref, w_ref = refs[0], refs[1]
    pos = 2
    z_ref = b_ref = res_ref = None
    if act == "glu":
        z_ref = refs[pos]
        pos += 1
    if act in ("glu", "scale"):
        b_ref = refs[pos]
        pos += 1
    if has_res:
        res_ref = refs[pos]
        pos += 1
    o_ref = refs[pos]
    acc_ref = refs[pos + 1] if nk > 1 else None

    def epilogue(y):
        if act == "relu2":
            y = jnp.square(jnp.maximum(y, 0.0))
        elif act == "glu":
            y = z_ref[...].astype(F32) * _sigmoid(y + b_ref[...])
        elif act == "scale":
            y = y * b_ref[...]
        if has_res:
            y = y + res_ref[...]
        o_ref[...] = y.astype(o_ref.dtype)

    part = _dot(a_ref[...], w_ref[...])
    if nk == 1:
        epilogue(part)
        return
    k = pl.program_id(2)

    @pl.when(k == 0)
    def _():
        acc_ref[...] = part

    @pl.when(k > 0)
    def _():
        acc_ref[...] += part

    @pl.when(k == nk - 1)
    def _():
        epilogue(acc_ref[...])


def _matmul(a, w, *, out_dtype, act=None, z=None, bias=None, res=None, tm=1024, tn=1024, tk=2048):
    m, kdim = a.shape
    n = w.shape[1]
    tm, tn, tk = min(tm, m), min(tn, n), min(tk, kdim)
    assert m % tm == 0 and n % tn == 0 and kdim % tk == 0, (a.shape, w.shape, tm, tn, tk)
    nk = kdim // tk
    in_specs = [pl.BlockSpec((tm, tk), lambda i, j, k: (i, k)),
                pl.BlockSpec((tk, tn), lambda i, j, k: (k, j))]
    args = [a, w]
    if act == "glu":
        in_specs.append(pl.BlockSpec((tm, tn), lambda i, j, k: (i, j)))
        args.append(z)
    if act in ("glu", "scale"):
        in_specs.append(pl.BlockSpec((1, tn), lambda i, j, k: (0, j)))
        args.append(bias.reshape(1, n).astype(F32))
    if res is not None:
        in_specs.append(pl.BlockSpec((tm, tn), lambda i, j, k: (i, j)))
        args.append(res)
    scratch = [pltpu.VMEM((tm, tn), F32)] if nk > 1 else []
    return pl.pallas_call(
        functools.partial(_mm_body, nk=nk, act=act, has_res=res is not None),
        grid=(m // tm, n // tn, nk),
        in_specs=in_specs,
        out_specs=pl.BlockSpec((tm, tn), lambda i, j, k: (i, j)),
        out_shape=jax.ShapeDtypeStruct((m, n), out_dtype),
        scratch_shapes=scratch,
        compiler_params=_params("parallel", "parallel", "arbitrary"), name="matmul",
    )(*args)


def _mm_cat_body(a1_ref, a2_ref, w1_ref, w2_ref, res_ref, o_ref):
    y = _dot(a1_ref[...], w1_ref[...]) + _dot(a2_ref[...], w2_ref[...])
    o_ref[...] = (y + res_ref[...]).astype(o_ref.dtype)


def _matmul_cat(a1, a2, w, res, tm=1024, tn=1024):
    m, k1 = a1.shape
    k2 = a2.shape[1]
    n = w.shape[1]
    tm, tn = min(tm, m), min(tn, n)
    assert m % tm == 0 and n % tn == 0 and w.shape[0] == k1 + k2
    tile = pl.BlockSpec((tm, tn), lambda i, j: (i, j))
    return pl.pallas_call(
        _mm_cat_body, grid=(m // tm, n // tn),
        in_specs=[pl.BlockSpec((tm, k1), lambda i, j: (i, 0)),
                  pl.BlockSpec((tm, k2), lambda i, j: (i, 0)),
                  pl.BlockSpec((k1, tn), lambda i, j: (0, j)),
                  pl.BlockSpec((k2, tn), lambda i, j: (0, j)),
                  tile],
        out_specs=tile, out_shape=jax.ShapeDtypeStruct((m, n), res.dtype),
        compiler_params=_params("parallel", "parallel"), name="matmul_cat",
    )(a1, a2, w[:k1], w[k1:], res)


def _mem_attn_body(q_ref, k_ref, v_ref, o_ref):
    scale = HEAD_DIM ** -0.5
    for h in range(MEM_HEADS):
        sl = slice(h * HEAD_DIM, (h + 1) * HEAD_DIM)
        logits = _dot_nt(q_ref[0, :, sl], k_ref[0, :, sl]) * scale
        mx = jnp.max(logits, axis=-1, keepdims=True)
        e = jnp.exp(logits - mx)
        p = e / jnp.sum(e, axis=-1, keepdims=True)
        o_ref[0, :, sl] = _dot(p.astype(BF16), v_ref[0, :, sl]).astype(o_ref.dtype)


def _mem_attention(q, mem_k, mem_v, ts=1024):
    bsz, seq, _ = q.shape
    mlen = mem_k.shape[1]
    ts = min(ts, seq)
    return pl.pallas_call(
        _mem_attn_body, grid=(bsz, seq // ts),
        in_specs=[pl.BlockSpec((1, ts, MEM_WIDTH), lambda b, i: (b, i, 0)),
                  pl.BlockSpec((1, mlen, MEM_WIDTH), lambda b, i: (b, 0, 0)),
                  pl.BlockSpec((1, mlen, MEM_WIDTH), lambda b, i: (b, 0, 0))],
        out_specs=pl.BlockSpec((1, ts, MEM_WIDTH), lambda b, i: (b, i, 0)),
        out_shape=jax.ShapeDtypeStruct((bsz, seq, MEM_WIDTH), BF16),
        compiler_params=_params("parallel", "parallel"), name="mem_attention",
    )(q, mem_k, mem_v)


def _s5_build_body(lre_ref, lim_ref, ldt_ref, cre_ref, cim_ref, btr_ref, bti_ref,
                   tt_ref, wbr_ref, wbi_ref, car_ref, cai_ref, al_ref, r_ref, *, chunk):
    grp, pst = S5_GROUP, S5_STATE
    l_re, l_im = lre_ref[0], lim_ref[0]
    dt = jnp.exp(ldt_ref[0])
    x_re, x_im = l_re * dt, l_im * dt
    a_mag = jnp.exp(x_re)
    a_re, a_im = a_mag * jnp.cos(x_im), a_mag * jnp.sin(x_im)
    den = l_re * l_re + l_im * l_im
    z_re = ((a_re - 1.0) * l_re + a_im * l_im) / den
    z_im = (a_im * l_re - (a_re - 1.0) * l_im) / den
    bt_re, bt_im = btr_ref[0], bti_ref[0]
    bb_re = z_re * bt_re - z_im * bt_im
    bb_im = z_re * bt_im + z_im * bt_re
    c_re, c_im = cre_ref[0], cim_ref[0]

    def powers(offset):
        t = (lax.broadcasted_iota(jnp.int32, (chunk, 1, pst), 0) + offset).astype(F32)
        mag = jnp.exp(x_re[None] * t)
        ang = x_im[None] * t
        return mag * jnp.cos(ang), mag * jnp.sin(ang)

    def times(m_re, m_im, p_re, p_im):
        o_re = m_re[None] * p_re - m_im[None] * p_im
        o_im = m_re[None] * p_im + m_im[None] * p_re
        return o_re.reshape(chunk * grp, pst), o_im.reshape(chunk * grp, pst)

    p0_re, p0_im = powers(0)
    p1_re, p1_im = powers(1)
    ca_re, ca_im = times(c_re, c_im, p1_re, p1_im)
    car_ref[0] = ca_re.astype(car_ref.dtype)
    cai_ref[0] = ca_im.astype(cai_ref.dtype)
    rev = chunk - 1 - lax.broadcasted_iota(jnp.int32, (chunk, 1, pst), 0)
    trev = rev.astype(F32)
    mag = jnp.exp(x_re[None] * trev)
    ang = x_im[None] * trev
    wb_re, wb_im = times(bb_re, bb_im, mag * jnp.cos(ang), mag * jnp.sin(ang))
    wbr_ref[0] = wb_re.astype(wbr_ref.dtype)
    wbi_ref[0] = wb_im.astype(wbi_ref.dtype)
    lf = float(chunk)
    magl = jnp.exp(x_re * lf)
    al_ref[0] = jnp.concatenate([magl * jnp.cos(x_im * lf), magl * jnp.sin(x_im * lf)], axis=0)
    ce_re, ce_im = times(c_re, c_im, p0_re, p0_im)
    r_ref[...] = _dot_nt(bb_re, ce_re, HI) - _dot_nt(bb_im, ce_im, HI)
    width = chunk * grp
    lane = lax.broadcasted_iota(jnp.int32, (grp, width), 1)
    for j in range(chunk):
        r = r_ref[...]
        if j:
            r = jnp.where(lane >= j * grp, pltpu.roll(r, j * grp, axis=1), 0.0)
        tt_ref[0, j * grp:(j + 1) * grp, :] = r.astype(tt_ref.dtype)


def _s5_build(lam_re, lam_im, log_dt, b_re, b_im, c_re, c_im, chunk):
    ng, pst = lam_re.shape
    grp = S5_GROUP
    width = chunk * grp
    row = lambda x: x.astype(F32).reshape(ng, 1, pst)
    ldt = jnp.broadcast_to(log_dt.astype(F32)[:, None, None], (ng, 1, pst))
    args = (row(lam_re), row(lam_im), ldt, c_re.astype(F32), c_im.astype(F32),
            jnp.swapaxes(b_re.astype(F32), 1, 2), jnp.swapaxes(b_im.astype(F32), 1, 2))
    spec_row = pl.BlockSpec((1, 1, pst), lambda g: (g, 0, 0))
    spec_gp = pl.BlockSpec((1, grp, pst), lambda g: (g, 0, 0))
    spec_w = pl.BlockSpec((1, width, pst), lambda g: (g, 0, 0))
    return pl.pallas_call(
        functools.partial(_s5_build_body, chunk=chunk), grid=(ng,),
        in_specs=[spec_row, spec_row, spec_row, spec_gp, spec_gp, spec_gp, spec_gp],
        out_specs=[pl.BlockSpec((1, width, width), lambda g: (g, 0, 0)),
                   spec_w, spec_w, spec_w, spec_w,
                   pl.BlockSpec((1, 2, pst), lambda g: (g, 0, 0))],
        out_shape=[jax.ShapeDtypeStruct((ng, width, width), BF16),
                   jax.ShapeDtypeStruct((ng, width, pst), BF16),
                   jax.ShapeDtypeStruct((ng, width, pst), BF16),
                   jax.ShapeDtypeStruct((ng, width, pst), BF16),
                   jax.ShapeDtypeStruct((ng, width, pst), BF16),
                   jax.ShapeDtypeStruct((ng, 2, pst), F32)],
        scratch_shapes=[pltpu.VMEM((grp, width), F32)],
        compiler_params=_params("parallel"), name="s5_build",
    )(*args)


def _s5_scan_body(u_ref, tt_ref, wbr_ref, wbi_ref, car_ref, cai_ref, al_ref, y_ref,
                  sre_ref, sim_ref, *, bsz, nc):
    u = u_ref[0]
    sre_ref[...] = _dot(u, wbr_ref[0])
    sim_ref[...] = _dot(u, wbi_ref[0])
    a_re, a_im = al_ref[0, 0:1, :], al_ref[0, 1:2, :]

    def step(c, carry):
        out = []
        for b in range(bsz):
            h_re, h_im = carry[b]
            r = b * nc + c
            loc_re, loc_im = sre_ref[pl.ds(r, 1), :], sim_ref[pl.ds(r, 1), :]
            sre_ref[pl.ds(r, 1), :] = h_re
            sim_ref[pl.ds(r, 1), :] = h_im
            out.append((a_re * h_re - a_im * h_im + loc_re, a_re * h_im + a_im * h_re + loc_im))
        return tuple(out)

    zero = jnp.zeros((1, S5_STATE), F32)
    lax.fori_loop(0, nc, step, tuple((zero, zero) for _ in range(bsz)))
    y = _dot(u, tt_ref[0])
    y = y + _dot_nt(sre_ref[...].astype(BF16), car_ref[0])
    y = y - _dot_nt(sim_ref[...].astype(BF16), cai_ref[0])
    y_ref[0] = y


def _s5_scan(u_g, tables, bsz):
    tt, wbr, wbi, car, cai, al = tables
    ng, rows, width = u_g.shape
    pst = S5_STATE
    nc = rows // bsz
    spec_w = pl.BlockSpec((1, width, pst), lambda g: (g, 0, 0))
    return pl.pallas_call(
        functools.partial(_s5_scan_body, bsz=bsz, nc=nc), grid=(ng,),
        in_specs=[pl.BlockSpec((1, rows, width), lambda g: (g, 0, 0)),
                  pl.BlockSpec((1, width, width), lambda g: (g, 0, 0)),
                  spec_w, spec_w, spec_w, spec_w,
                  pl.BlockSpec((1, 2, pst), lambda g: (g, 0, 0))],
        out_specs=pl.BlockSpec((1, rows, width), lambda g: (g, 0, 0)),
        out_shape=jax.ShapeDtypeStruct((ng, rows, width), F32),
        scratch_shapes=[pltpu.VMEM((rows, pst), F32), pltpu.VMEM((rows, pst), F32)],
        compiler_params=_params("parallel"), name="s5_scan",
    )(u_g, tt, wbr, wbi, car, cai, al)


def _s5_act_body(y_ref, u_ref, d_ref, o_ref):
    o_ref[...] = jax.nn.gelu(y_ref[...] + d_ref[...] * u_ref[...]).astype(o_ref.dtype)


def _s5_act(y, u, d_skip, tm=512):
    m, w = y.shape
    tm = min(tm, m)
    spec = pl.BlockSpec((tm, w), lambda i: (i, 0))
    return pl.pallas_call(
        _s5_act_body, grid=(m // tm,),
        in_specs=[spec, spec, pl.BlockSpec((1, w), lambda i: (0, 0))],
        out_specs=spec, out_shape=jax.ShapeDtypeStruct((m, w), BF16),
        compiler_params=_params("parallel"), name="s5_act",
    )(y, u, d_skip.reshape(1, w).astype(F32))


def _s5_mixer(u, lam_re, lam_im, log_dt, b_re, b_im, c_re, c_im, d_skip, w_glu, b_glu, bsz):
    m, width = u.shape
    seq = m // bsz
    ng = width // S5_GROUP
    chunk = min(S5_CHUNK, seq)
    nc = seq // chunk
    tables = _s5_build(lam_re, lam_im, log_dt, b_re, b_im, c_re, c_im, chunk)
    u_g = u.reshape(bsz * nc, chunk, ng, S5_GROUP).transpose(2, 0, 1, 3)
    u_g = u_g.reshape(ng, bsz * nc, chunk * S5_GROUP).astype(BF16)
    y_g = _s5_scan(u_g, tables, bsz)
    y = y_g.reshape(ng, bsz * nc, chunk, S5_GROUP).transpose(1, 2, 0, 3).reshape(m, width)
    z = _s5_act(y, u, d_skip)
    return _matmul(z, w_glu.astype(BF16), out_dtype=BF16, act="glu", z=z, bias=b_glu,
                   tn=min(width, 768), tk=width)


def _gdn_conv_body(x_ref, halo_ref, w_ref, o_ref, buf_ref, *, ts, nh):
    i = pl.program_id(1)
    j = pl.program_id(2)
    halo = halo_ref[0]
    buf_ref[0:8, :] = jnp.where(i == 0, jnp.zeros_like(halo), halo)
    buf_ref[8:8 + ts, :] = x_ref[0]
    acc = None
    for tap in range(GDN_CONV):
        off = 8 - (GDN_CONV - 1) + tap
        term = buf_ref[off:off + ts, :] * w_ref[tap:tap + 1, :]
        acc = term if acc is None else acc + term
    y = acc * _sigmoid(acc)
    qscale = jnp.where(j == 0, HEAD_DIM ** -0.5, 1.0).astype(F32)
    for h in range(nh):
        sl = slice(h * HEAD_DIM, (h + 1) * HEAD_DIM)
        yh = y[:, sl]
        nrm = yh * lax.rsqrt(jnp.sum(yh * yh, axis=-1, keepdims=True) + EPS) * qscale
        o_ref[0, :, sl] = jnp.where(j == 2, yh, nrm)


def _gdn_conv(x, conv_w, ts=256):
    bsz, seq, w4 = x.shape
    wd = w4 // 4
    w3 = 3 * wd
    ts = min(ts, seq)
    per8 = ts // 8
    return pl.pallas_call(
        functools.partial(_gdn_conv_body, ts=ts, nh=wd // HEAD_DIM),
        grid=(bsz, seq // ts, 3),
        in_specs=[pl.BlockSpec((1, ts, wd), lambda b, i, j: (b, i, j)),
                  pl.BlockSpec((1, 8, wd), lambda b, i, j: (b, jnp.maximum(i * per8 - 1, 0), j)),
                  pl.BlockSpec((GDN_CONV, wd), lambda b, i, j: (0, j))],
        out_specs=pl.BlockSpec((1, ts, wd), lambda b, i, j: (b, i, j)),
        out_shape=jax.ShapeDtypeStruct((bsz, seq, w3), F32),
        scratch_shapes=[pltpu.VMEM((ts + 8, wd), F32)],
        compiler_params=_params("parallel", "parallel", "parallel"), name="gdn_conv",
    )(x, x, conv_w.astype(F32))


_G_GC, _G_EGC, _G_EDEC, _G_BETA, _G_GLAST = 0, 16, 32, 48, 64


def _gdn_gates_body(x_ref, alog_ref, dtb_ref, o_ref, *, ts):
    x = x_ref[...]
    g = -jnp.exp(alog_ref[...]) * _softplus(x + dtb_ref[...])
    r = lax.broadcasted_iota(jnp.int32, (ts, ts), 0)
    c = lax.broadcasted_iota(jnp.int32, (ts, ts), 1)
    same = (r // GDN_CHUNK) == (c // GDN_CHUNK)
    gc = _dot(jnp.where(same & (r >= c), 1.0, 0.0).astype(F32), g, HI)
    gl = _dot(jnp.where(same, 1.0, 0.0).astype(F32), g, HI)
    lane = lax.broadcasted_iota(jnp.int32, x.shape, 1)
    out = jnp.where(lane < _G_EGC, gc,
          jnp.where(lane < _G_EDEC, jnp.exp(gc),
          jnp.where(lane < _G_BETA, jnp.exp(gl - gc),
          jnp.where(lane < _G_GLAST, _sigmoid(x), jnp.exp(gl)))))
    o_ref[...] = out


def _gdn_gates(ab, a_log, dt_bias, ts=512):
    m = ab.shape[0]
    ts = min(ts, m)
    nh = a_log.shape[0]

    def lanes(p):
        row = jnp.zeros((LANES,), F32)
        for off in (_G_GC, _G_EGC, _G_EDEC, _G_GLAST):
            row = row.at[off:off + nh].set(p.astype(F32))
        return row.reshape(1, LANES)

    spec = pl.BlockSpec((ts, LANES), lambda i: (i, 0))
    prm = pl.BlockSpec((1, LANES), lambda i: (0, 0))
    return pl.pallas_call(
        functools.partial(_gdn_gates_body, ts=ts), grid=(m // ts,),
        in_specs=[spec, prm, prm], out_specs=spec,
        out_shape=jax.ShapeDtypeStruct((m, LANES), F32),
        compiler_params=_params("parallel"), name="gdn_gates",
    )(ab, lanes(a_log), lanes(dt_bias))


def _split3(x):
    hi = x.astype(BF16)
    lo = (x - hi.astype(F32)).astype(BF16)
    return hi, lo


def _dot3(a, b):
    a_hi, a_lo = _split3(a)
    b_hi, b_lo = _split3(b)
    lhs = jnp.concatenate([a_hi, a_lo, a_hi], axis=1)
    rhs = jnp.concatenate([b_hi, b_hi, b_lo], axis=0)
    return _dot(lhs, rhs)


def _gdn_intra_body(q_ref, k_ref, v_ref, gp_ref, gct_ref, u_ref, w_ref, qd_ref, kd_ref, at_ref,
                    *, ngrp, grows):
    cl = GDN_CHUNK
    h = pl.program_id(1)
    gp = gp_ref[0]
    gc_col = _lane_pick(gp, h + _G_GC)
    egc_col = _lane_pick(gp, h + _G_EGC)
    edec_col = _lane_pick(gp, h + _G_EDEC)
    beta_col = _lane_pick(gp, h + _G_BETA)
    r = lax.broadcasted_iota(jnp.int32, (grows, grows), 0)
    c = lax.broadcasted_iota(jnp.int32, (grows, grows), 1)
    same = (r // cl) == (c // cl)
    lower = same & (r >= c)
    strict = same & (r > c)
    steps = int(math.log2(cl))
    for g in range(ngrp):
        rows = slice(g * grows, (g + 1) * grows)
        q, k, v = q_ref[0, rows, :], k_ref[0, rows, :], v_ref[0, rows, :]
        beta, egc = beta_col[rows], egc_col[rows]
        gc_row = gct_ref[0, 0, :, rows]
        decay = jnp.exp(jnp.where(lower, gc_col[rows] - gc_row, -jnp.inf))
        kb, vb = k * beta, v * beta
        kbf = k.astype(BF16)
        p = -jnp.where(strict, _dot_nt(kb.astype(BF16), kbf) * decay, 0.0)
        sol = jnp.concatenate([vb, kb * egc], axis=-1)
        for s in range(steps):
            sol = sol + _dot3(p, sol)
            if s + 1 < steps:
                p = _dot3(p, p)
        u_ref[0, rows, :] = sol[:, :HEAD_DIM]
        w_ref[0, rows, :] = sol[:, HEAD_DIM:]
        qd_ref[0, rows, :] = q * egc
        kd_ref[0, rows, :] = k * edec_col[rows]
        attn = jnp.where(lower, _dot_nt(q.astype(BF16), kbf) * decay, 0.0)
        for n in range(grows // cl):
            at_ref[0, 0, g * grows + n * cl:g * grows + (n + 1) * cl, :] = (
                attn[n * cl:(n + 1) * cl, n * cl:(n + 1) * cl])


def _gdn_intra(qkv, gates, gct, ngrp=2, grows=256):
    bsz, seq, w3 = qkv.shape
    wd = w3 // 3
    nh = wd // HEAD_DIM
    cl = GDN_CHUNK
    grows = min(grows, seq)
    ngrp = min(ngrp, seq // grows)
    tt = ngrp * grows
    col = lambda off: pl.BlockSpec((1, tt, HEAD_DIM), lambda b, h, i: (b, i, h + off))
    out_sd = jax.ShapeDtypeStruct((bsz, seq, wd), F32)
    return pl.pallas_call(
        functools.partial(_gdn_intra_body, ngrp=ngrp, grows=grows), grid=(bsz, nh, seq // tt),
        in_specs=[col(0), col(nh), col(2 * nh),
                  pl.BlockSpec((1, tt, LANES), lambda b, h, i: (b, i, 0)),
                  pl.BlockSpec((1, 1, 1, tt), lambda b, h, i: (b, h, 0, i))],
        out_specs=[col(0), col(0), col(0), col(0),
                   pl.BlockSpec((1, 1, tt, cl), lambda b, h, i: (b, h, i, 0))],
        out_shape=[out_sd, out_sd, out_sd, out_sd,
                   jax.ShapeDtypeStruct((bsz, nh, seq, cl), F32)],
        compiler_params=_params("parallel", "parallel", "parallel"), name="gdn_intra",
    )(qkv, qkv, qkv, gates, gct)


def _gdn_scan_body(gl_ref, u_ref, w_ref, qd_ref, kd_ref, at_ref, gate_ref, on_ref, o_ref,
                   state_ref, *, nb, nh):
    cl = GDN_CHUNK
    b = pl.program_id(0)
    i = pl.program_id(1)

    @pl.when(i == 0)
    def _():
        state_ref[...] = jnp.zeros_like(state_ref)

    for n in range(nb):
        rows = slice(n * cl, (n + 1) * cl)
        for h in range(nh):
            sl = slice(h * HEAD_DIM, (h + 1) * HEAD_DIM)
            state = state_ref[h]
            sb = state.astype(BF16)
            v_new = u_ref[0, rows, sl] - _dot(w_ref[0, rows, sl].astype(BF16), sb)
            vb = v_new.astype(BF16)
            out = _dot(qd_ref[0, rows, sl].astype(BF16), sb) + _dot(at_ref[0, h, rows, :].astype(BF16), vb)
            g_last = gl_ref[b, i * nb + n, h]
            state_ref[h] = state * g_last + _dot_tn(kd_ref[0, rows, sl].astype(BF16), vb)
            gate = gate_ref[0, rows, sl]
            nrm = out * lax.rsqrt(jnp.mean(out * out, axis=-1, keepdims=True) + EPS) * on_ref[...]
            o_ref[0, rows, sl] = (nrm * (gate * _sigmoid(gate))).astype(o_ref.dtype)


def _gdn_scan(glast, u_c, w_c, q_dec, k_dec, attn, main, o_norm, nb=2):
    bsz, seq, wd = u_c.shape
    nh = wd // HEAD_DIM
    cl = GDN_CHUNK
    nb = min(nb, seq // cl)
    tt = nb * cl
    tok = pl.BlockSpec((1, tt, wd), lambda b, i: (b, i, 0))
    return pl.pallas_call(
        functools.partial(_gdn_scan_body, nb=nb, nh=nh), grid=(bsz, seq // tt),
        in_specs=[pl.BlockSpec(memory_space=pltpu.SMEM), tok, tok, tok, tok,
                  pl.BlockSpec((1, nh, tt, cl), lambda b, i: (b, 0, i, 0)),
                  pl.BlockSpec((1, tt, wd), lambda b, i: (b, i, 3)),
                  pl.BlockSpec((1, HEAD_DIM), lambda b, i: (0, 0))],
        out_specs=tok,
        out_shape=jax.ShapeDtypeStruct((bsz, seq, wd), BF16),
        scratch_shapes=[pltpu.VMEM((nh, HEAD_DIM, HEAD_DIM), F32)],
        compiler_params=_params("parallel", "arbitrary"), name="gdn_scan",
    )(glast, u_c, w_c, q_dec, k_dec, attn, main, o_norm.reshape(1, HEAD_DIM).astype(F32))


def _gdn_mixer(main, ab, conv_w, a_log, dt_bias, o_norm):
    bsz, seq, w4 = main.shape
    nh = w4 // 4 // HEAD_DIM
    qkv = _gdn_conv(main, conv_w)
    gates = _gdn_gates(ab, a_log, dt_bias).reshape(bsz, seq, LANES)
    gct = gates[:, :, _G_GC:_G_GC + nh].transpose(0, 2, 1).reshape(bsz, nh, 1, seq)
    glast = gates[:, GDN_CHUNK - 1::GDN_CHUNK, _G_GLAST:_G_GLAST + nh]
    u_c, w_c, q_dec, k_dec, attn = _gdn_intra(qkv, gates, gct)
    return _gdn_scan(glast, u_c, w_c, q_dec, k_dec, attn, main, o_norm)


def _fox_cumf_body(x_ref, bf_ref, o_ref, carry_ref, *, ts):
    @pl.when(pl.program_id(1) == 0)
    def _():
        carry_ref[...] = jnp.zeros_like(carry_ref)

    ls = -_softplus(-(x_ref[0] + bf_ref[...]))
    r = lax.broadcasted_iota(jnp.int32, (ts, ts), 0)
    c = lax.broadcasted_iota(jnp.int32, (ts, ts), 1)
    cum = _dot(jnp.where(r >= c, 1.0, 0.0).astype(F32), ls, HI) + carry_ref[...]
    o_ref[0] = cum
    carry_ref[...] = cum[ts - 1:ts, :]


def _fox_cumf(f_logit, b_f, ts=256):
    bsz, seq, _ = f_logit.shape
    ts = min(ts, seq)
    bf = jnp.zeros((LANES,), F32).at[:b_f.shape[0]].set(b_f.astype(F32)).reshape(1, LANES)
    spec = pl.BlockSpec((1, ts, LANES), lambda b, i: (b, i, 0))
    return pl.pallas_call(
        functools.partial(_fox_cumf_body, ts=ts), grid=(bsz, seq // ts),
        in_specs=[spec, pl.BlockSpec((1, LANES), lambda b, i: (0, 0))],
        out_specs=spec, out_shape=jax.ShapeDtypeStruct((bsz, seq, LANES), F32),
        scratch_shapes=[pltpu.VMEM((1, LANES), F32)],
        compiler_params=_params("parallel", "arbitrary"), name="fox_cumf",
    )(f_logit, bf)


LOG2E = math.log2(math.e)
FOX_QSCALE = HEAD_DIM ** -0.5 * LOG2E


def _fox_attn_body(q_ref, k_ref, v_ref, cft_ref, o_ref, m_ref, l_ref, acc_ref, *, tq):
    qi = pl.program_id(2)
    th = tq // 2
    m_ref[...] = jnp.full_like(m_ref, -jnp.inf)
    l_ref[...] = jnp.zeros_like(l_ref)
    acc_ref[...] = jnp.zeros_like(acc_ref)

    def update(half, kb, vb, ck, masked):
        rows = slice(half * th, (half + 1) * th)
        s = _dot_nt(q_ref[0, rows, :], kb) - ck
        if masked:
            r = lax.broadcasted_iota(jnp.int32, s.shape, 0) + half * th
            c = lax.broadcasted_iota(jnp.int32, s.shape, 1)
            s = jnp.where(r >= c, s, -jnp.inf)
        m_prev = m_ref[rows, :]
        m_new = jnp.maximum(m_prev, jnp.max(s, axis=-1, keepdims=True))
        alpha = jnp.exp2(m_prev - m_new)
        p = jnp.exp2(s - m_new)
        l_ref[rows, :] = alpha * l_ref[rows, :] + jnp.sum(p, axis=-1, keepdims=True)
        acc_ref[rows, :] = alpha * acc_ref[rows, :] + _dot(p.astype(BF16), vb)
        m_ref[rows, :] = m_new

    def body(j, carry):
        start = pl.multiple_of(j * tq, tq)
        kb, vb = k_ref[0, pl.ds(start, tq), :], v_ref[0, pl.ds(start, tq), :]
        ck = cft_ref[0, 0, j] * LOG2E
        update(0, kb, vb, ck, False)
        update(1, kb, vb, ck, False)
        return carry

    lax.fori_loop(0, qi, body, 0)
    d0 = pl.multiple_of(qi * tq, tq)
    ck = cft_ref[0, 0, qi] * LOG2E
    update(0, k_ref[0, pl.ds(d0, th), :], v_ref[0, pl.ds(d0, th), :], ck[:, :th], True)
    update(1, k_ref[0, pl.ds(d0, tq), :], v_ref[0, pl.ds(d0, tq), :], ck, True)
    o_ref[0] = (acc_ref[...] / l_ref[...]).astype(o_ref.dtype)


def _fox_attention(qkv, cumf_t, tq):
    bsz, seq, w3 = qkv.shape
    wd = w3 // 3
    nh = wd // HEAD_DIM
    nq = seq // tq
    kv = lambda off: pl.BlockSpec((1, seq, HEAD_DIM), lambda b, h, qi: (b, 0, h + off))
    return pl.pallas_call(
        functools.partial(_fox_attn_body, tq=tq), grid=(bsz, nh, nq),
        in_specs=[pl.BlockSpec((1, tq, HEAD_DIM), lambda b, h, qi: (b, qi, h)),
                  kv(nh), kv(2 * nh),
                  pl.BlockSpec((1, 1, nq, 1, tq), lambda b, h, qi: (b, h, 0, 0, 0))],
        out_specs=pl.BlockSpec((1, tq, HEAD_DIM), lambda b, h, qi: (b, qi, h)),
        out_shape=jax.ShapeDtypeStruct((bsz, seq, wd), BF16),
        scratch_shapes=[pltpu.VMEM((tq, 1), F32), pltpu.VMEM((tq, 1), F32),
                        pltpu.VMEM((tq, HEAD_DIM), F32)],
        compiler_params=_params("parallel", "parallel", "arbitrary"),
        name="fox_attention",
    )(qkv, qkv, qkv, cumf_t)


def _fox_mixer(qkv, f_logit, b_f, tq=1024):
    bsz, seq, w3 = qkv.shape
    nh = w3 // 3 // HEAD_DIM
    tq = min(tq, seq)
    cumf = _fox_cumf(f_logit, b_f)
    cumf_t = cumf[:, :, :nh].transpose(0, 2, 1).reshape(bsz, nh, seq // tq, 1, tq)
    return _fox_attention(qkv, cumf_t, tq)


def _pad_cols(w, groups):
    out = jnp.zeros((w.shape[0], LANES), w.dtype)
    for off in groups:
        out = out.at[:, off:off + w.shape[1]].set(w)
    return out


def kernel(x, mem, mem_norm, w_mem_kv, norm1, w_out, norm2, w_up, w_down, norm_f,
           s5_w_in, s5_lam_re, s5_lam_im, s5_log_dt, s5_b_re, s5_b_im, s5_c_re, s5_c_im,
           s5_d_skip, s5_w_glu, s5_b_glu,
           gdn_w_in, gdn_conv_w, gdn_a_log, gdn_dt_bias, gdn_o_norm,
           fox_w_in, fox_b_f):
    bsz, seq, d = x.shape
    m = bsz * seq
    depth = norm1.shape[0]
    wd = d - MEM_WIDTH
    nh = wd // HEAD_DIM
    mlen = mem.shape[1]

    mem_a = _rmsnorm(mem.reshape(bsz * mlen, d), mem_norm)
    mkv = _matmul(mem_a, w_mem_kv.astype(BF16), out_dtype=BF16, tn=512)
    mem_k = mkv[:, :MEM_WIDTH].reshape(bsz, mlen, MEM_WIDTH)
    mem_v = mkv[:, MEM_WIDTH:].reshape(bsz, mlen, MEM_WIDTH)

    h = x.reshape(m, d)
    for i in range(depth):
        kind, j = i % 3, i // 3
        a = _rmsnorm(h, norm1[i])
        if kind == 0:
            w_in = s5_w_in[j].astype(BF16)
            u = _matmul(a, w_in[:, :wd], out_dtype=F32, tn=768)
            mq = _matmul(a, w_in[:, wd:], out_dtype=BF16, tn=512)
            mix = _s5_mixer(u, s5_lam_re[j], s5_lam_im[j], s5_log_dt[j], s5_b_re[j], s5_b_im[j],
                            s5_c_re[j], s5_c_im[j], s5_d_skip[j], s5_w_glu[j], s5_b_glu[j], bsz)
        elif kind == 1:
            w_in = gdn_w_in[j]
            w_main = w_in[:, :4 * wd].astype(BF16)
            w_a = w_in[:, 4 * wd:4 * wd + nh]
            w_b = w_in[:, 4 * wd + nh:4 * wd + 2 * nh]
            w_ab = (_pad_cols(w_a, (_G_GC, _G_EGC, _G_EDEC, _G_GLAST))
                    + _pad_cols(w_b, (_G_BETA,))).astype(BF16)
            main = _matmul(a, w_main, out_dtype=F32, tn=768)
            ab = _matmul(a, w_ab, out_dtype=F32)
            mq = _matmul(a, w_in[:, -MEM_WIDTH:].astype(BF16), out_dtype=BF16, tn=512)
            main = main.reshape(bsz, seq, 4 * wd)
            mix = _gdn_mixer(main, ab, gdn_conv_w[j], gdn_a_log[j], gdn_dt_bias[j],
                             gdn_o_norm[j]).reshape(m, wd)
        else:
            w_in = fox_w_in[j]
            qscale = jnp.concatenate([jnp.full((wd,), FOX_QSCALE, F32), jnp.ones((2 * wd,), F32)])
            qkv = _matmul(a, w_in[:, :3 * wd].astype(BF16), out_dtype=BF16, act="scale",
                          bias=qscale, tn=768)
            fl = _matmul(a, _pad_cols(w_in[:, 3 * wd:3 * wd + nh], (0,)).astype(BF16), out_dtype=F32)
            mq = _matmul(a, w_in[:, -MEM_WIDTH:].astype(BF16), out_dtype=BF16, tn=512)
            mix = _fox_mixer(qkv.reshape(bsz, seq, 3 * wd), fl.reshape(bsz, seq, LANES),
                             fox_b_f[j]).reshape(m, wd)
        read = _mem_attention(mq.reshape(bsz, seq, MEM_WIDTH), mem_k, mem_v).reshape(m, MEM_WIDTH)
        h = _matmul_cat(mix, read, w_out[i].astype(BF16), h)
        a = _rmsnorm(h, norm2[i])
        up = _matmul(a, w_up[i].astype(BF16), out_dtype=BF16, act="relu2")
        h = _matmul(up, w_down[i].astype(BF16), out_dtype=F32, res=h)
    return _rmsnorm(h, norm_f, out_dtype=x.dtype).reshape(bsz, seq, d)
```

```python
import functools
import math

import jax
import jax.numpy as jnp
from jax import lax
from jax.experimental import pallas as pl
from jax.experimental.pallas import tpu as pltpu

F32 = jnp.float32
BF16 = jnp.bfloat16
EPS = 1e-6
HEAD_DIM = 128
LANES = 128
MEM_HEADS = 4
MEM_WIDTH = MEM_HEADS * HEAD_DIM
S5_GROUP = 16
S5_STATE = 64
S5_CHUNK = 64
GDN_CHUNK = 64
GDN_CONV = 4
VMEM_LIMIT_BYTES = 48 * 1024 * 1024
HI = lax.Precision.HIGHEST


def _params(*sem):
    return pltpu.CompilerParams(dimension_semantics=sem, vmem_limit_bytes=VMEM_LIMIT_BYTES)


def _dot(a, b, precision=None):
    return jnp.dot(a, b, preferred_element_type=F32, precision=precision)


def _dot_nt(a, b, precision=None):
    return lax.dot_general(a, b, (((1,), (1,)), ((), ())), preferred_element_type=F32,
                           precision=precision)


def _dot_tn(a, b, precision=None):
    return lax.dot_general(a, b, (((0,), (0,)), ((), ())), preferred_element_type=F32,
                           precision=precision)


def _softplus(x):
    return jnp.maximum(x, 0.0) + jnp.log1p(jnp.exp(-jnp.abs(x)))


def _sigmoid(x):
    return 1.0 / (1.0 + jnp.exp(-x))


def _lane_pick(x, lane):
    ids = lax.broadcasted_iota(jnp.int32, x.shape, 1)
    return jnp.sum(jnp.where(ids == lane, x, 0.0), axis=-1, keepdims=True)


def _rmsnorm_body(x_ref, g_ref, o_ref):
    x = x_ref[...].astype(F32)
    ms = jnp.mean(x * x, axis=-1, keepdims=True)
    o_ref[...] = (x * lax.rsqrt(ms + EPS) * g_ref[...]).astype(o_ref.dtype)


def _rmsnorm(x2d, gain, out_dtype=BF16, tm=512):
    m, d = x2d.shape
    tm = min(tm, m)
    return pl.pallas_call(
        _rmsnorm_body, grid=(m // tm,),
        in_specs=[pl.BlockSpec((tm, d), lambda i: (i, 0)),
                  pl.BlockSpec((1, d), lambda i: (0, 0))],
        out_specs=pl.BlockSpec((tm, d), lambda i: (i, 0)),
        out_shape=jax.ShapeDtypeStruct((m, d), out_dtype),
        compiler_params=_params("parallel"), name="rmsnorm",
    )(x2d, gain.reshape(1, d).astype(F32))


def _mm_body(*refs, nk, act, has_res):
    a_ref, w_ref = refs[0], refs[1]
    pos = 2
    z_ref = b_ref = res_ref = None
    if act == "glu":
        z_ref = refs[pos]
        pos += 1
    if act in ("glu", "scale"):
        b_ref = refs[pos]
        pos += 1
    if has_res:
        res_ref = refs[pos]
        pos += 1
    o_ref = refs[pos]
    acc_ref = refs[pos + 1] if nk > 1 else None

    def epilogue(y):
        if act == "relu2":
            y = jnp.square(jnp.maximum(y, 0.0))
        elif act == "glu":
            y = z_ref[...].astype(F32) * _sigmoid(y + b_ref[...])
        elif act == "scale":
            y = y * b_ref[...]
        if has_res:
            y = y + res_ref[...]
        o_ref[...] = y.astype(o_ref.dtype)

    part = _dot(a_ref[...], w_ref[...])
    if nk == 1:
        epilogue(part)
        return
    k = pl.program_id(2)

    @pl.when(k == 0)
    def _():
        acc_ref[...] = part

    @pl.when(k > 0)
    def _():
        acc_ref[...] += part

    @pl.when(k == nk - 1)
    def _():
        epilogue(acc_ref[...])


def _matmul(a, w, *, out_dtype, layer=0, col0=0, n=None, act=None, z=None, bias=None, res=None,
            tm=1024, tn=1024, tk=2048):
    m, kdim = a.shape
    n = w.shape[2] if n is None else n
    tm, tn, tk = min(tm, m), min(tn, n), min(tk, kdim)
    assert m % tm == 0 and n % tn == 0 and kdim % tk == 0, (a.shape, w.shape, tm, tn, tk)
    nk = kdim // tk
    in_specs = [pl.BlockSpec((tm, tk), lambda i, j, k: (i, k)),
                pl.BlockSpec((None, tk, tn), lambda i, j, k: (layer, k, j + col0))]
    args = [a, w]
    if act == "glu":
        in_specs.append(pl.BlockSpec((tm, tn), lambda i, j, k: (i, j)))
        args.append(z)
    if act in ("glu", "scale"):
        in_specs.append(pl.BlockSpec((1, tn), lambda i, j, k: (0, j)))
        args.append(bias.reshape(1, n).astype(F32))
    if res is not None:
        in_specs.append(pl.BlockSpec((tm, tn), lambda i, j, k: (i, j)))
        args.append(res)
    scratch = [pltpu.VMEM((tm, tn), F32)] if nk > 1 else []
    return pl.pallas_call(
        functools.partial(_mm_body, nk=nk, act=act, has_res=res is not None),
        grid=(m // tm, n // tn, nk),
        in_specs=in_specs,
        out_specs=pl.BlockSpec((tm, tn), lambda i, j, k: (i, j)),
        out_shape=jax.ShapeDtypeStruct((m, n), out_dtype),
        scratch_shapes=scratch,
        compiler_params=_params("parallel", "parallel", "arbitrary"), name="matmul",
    )(*args)


def _norm_rows(y, gain):
    return y * lax.rsqrt(jnp.mean(y * y, axis=-1, keepdims=True) + EPS) * gain


def _mm_res_norm_body(a_ref, w_ref, res_ref, g_ref, *refs, nk, keep_h):
    acc_ref, norm_ref = refs if keep_h else refs[::-1]
    k = pl.program_id(1)

    @pl.when(k == 0)
    def _():
        acc_ref[...] = res_ref[...] + _dot(a_ref[...], w_ref[...])

    @pl.when(k > 0)
    def _():
        acc_ref[...] += _dot(a_ref[...], w_ref[...])

    @pl.when(k == nk - 1)
    def _():
        norm_ref[...] = _norm_rows(acc_ref[...], g_ref[...]).astype(norm_ref.dtype)


def _matmul_res_norm(a, w, layer, res, gain, *, norm_dtype, keep_h, tm=512, tk=2048):
    m, kdim = a.shape
    n = w.shape[2]
    tm, tk = min(tm, m), min(tk, kdim)
    assert m % tm == 0 and kdim % tk == 0
    nk = kdim // tk
    row = pl.BlockSpec((tm, n), lambda i, k: (i, 0))
    out_specs = [row] * (2 if keep_h else 1)
    out_shape = ([jax.ShapeDtypeStruct((m, n), F32)] if keep_h else []) + [
        jax.ShapeDtypeStruct((m, n), norm_dtype)]
    outs = pl.pallas_call(
        functools.partial(_mm_res_norm_body, nk=nk, keep_h=keep_h), grid=(m // tm, nk),
        in_specs=[pl.BlockSpec((tm, tk), lambda i, k: (i, k)),
                  pl.BlockSpec((None, tk, n), lambda i, k: (layer, k, 0)),
                  row, pl.BlockSpec((1, n), lambda i, k: (0, 0))],
        out_specs=out_specs, out_shape=out_shape,
        scratch_shapes=[] if keep_h else [pltpu.VMEM((tm, n), F32)],
        compiler_params=_params("parallel", "arbitrary"), name="matmul_res_norm",
    )(a, w, res, gain.reshape(1, n).astype(F32))
    return outs if keep_h else (None, outs[0])


def _mm_cat_norm_body(a1_ref, a2_ref, w1_ref, w2_ref, res_ref, g_ref, h_ref, n_ref):
    y = _dot(a1_ref[...], w1_ref[...]) + _dot(a2_ref[...], w2_ref[...]) + res_ref[...]
    h_ref[...] = y
    n_ref[...] = _norm_rows(y, g_ref[...]).astype(n_ref.dtype)


def _matmul_cat_norm(a1, a2, w, layer, res, gain, tm=512):
    m, k1 = a1.shape
    k2 = a2.shape[1]
    n = w.shape[2]
    tm = min(tm, m)
    assert m % tm == 0 and w.shape[1] == k1 + k2 and k1 % k2 == 0
    row = pl.BlockSpec((tm, n), lambda i: (i, 0))
    return pl.pallas_call(
        _mm_cat_norm_body, grid=(m // tm,),
        in_specs=[pl.BlockSpec((tm, k1), lambda i: (i, 0)),
                  pl.BlockSpec((tm, k2), lambda i: (i, 0)),
                  pl.BlockSpec((None, k1, n), lambda i: (layer, 0, 0)),
                  pl.BlockSpec((None, k2, n), lambda i: (layer, k1 // k2, 0)),
                  row, pl.BlockSpec((1, n), lambda i: (0, 0))],
        out_specs=[row, row],
        out_shape=[jax.ShapeDtypeStruct((m, n), F32), jax.ShapeDtypeStruct((m, n), BF16)],
        compiler_params=_params("parallel"), name="matmul_cat_norm",
    )(a1, a2, w, w, res, gain.reshape(1, n).astype(F32))


def _mem_attn_body(q_ref, k_ref, v_ref, o_ref):
    scale = HEAD_DIM ** -0.5
    for h in range(MEM_HEADS):
        sl = slice(h * HEAD_DIM, (h + 1) * HEAD_DIM)
        logits = _dot_nt(q_ref[0, :, sl], k_ref[0, :, sl]) * scale
        mx = jnp.max(logits, axis=-1, keepdims=True)
        e = jnp.exp(logits - mx)
        p = e / jnp.sum(e, axis=-1, keepdims=True)
        o_ref[0, :, sl] = _dot(p.astype(BF16), v_ref[0, :, sl]).astype(o_ref.dtype)


def _mem_attention(q, mem_k, mem_v, ts=1024):
    bsz, seq, _ = q.shape
    mlen = mem_k.shape[1]
    ts = min(ts, seq)
    return pl.pallas_call(
        _mem_attn_body, grid=(bsz, seq // ts),
        in_specs=[pl.BlockSpec((1, ts, MEM_WIDTH), lambda b, i: (b, i, 0)),
                  pl.BlockSpec((1, mlen, MEM_WIDTH), lambda b, i: (b, 0, 0)),
                  pl.BlockSpec((1, mlen, MEM_WIDTH), lambda b, i: (b, 0, 0))],
        out_specs=pl.BlockSpec((1, ts, MEM_WIDTH), lambda b, i: (b, i, 0)),
        out_shape=jax.ShapeDtypeStruct((bsz, seq, MEM_WIDTH), BF16),
        compiler_params=_params("parallel", "parallel"), name="mem_attention",
    )(q, mem_k, mem_v)


def _s5_build_body(lre_ref, lim_ref, ldt_ref, cre_ref, cim_ref, btr_ref, bti_ref,
                   tt_ref, wbr_ref, wbi_ref, car_ref, cai_ref, al_ref, r_ref, *, chunk):
    grp, pst = S5_GROUP, S5_STATE
    l_re, l_im = lre_ref[0], lim_ref[0]
    dt = jnp.exp(ldt_ref[0])
    x_re, x_im = l_re * dt, l_im * dt
    a_mag = jnp.exp(x_re)
    a_re, a_im = a_mag * jnp.cos(x_im), a_mag * jnp.sin(x_im)
    den = l_re * l_re + l_im * l_im
    z_re = ((a_re - 1.0) * l_re + a_im * l_im) / den
    z_im = (a_im * l_re - (a_re - 1.0) * l_im) / den
    bt_re, bt_im = btr_ref[0], bti_ref[0]
    bb_re = z_re * bt_re - z_im * bt_im
    bb_im = z_re * bt_im + z_im * bt_re
    c_re, c_im = cre_ref[0], cim_ref[0]

    sub = 8
    nhi = chunk // sub

    def closed_form(t):
        mag = jnp.exp(x_re[None] * t)
        ang = x_im[None] * t
        return mag * jnp.cos(ang), mag * jnp.sin(ang)

    def powers(reverse):
        a = lax.broadcasted_iota(jnp.int32, (nhi, 1, pst), 0)
        b = lax.broadcasted_iota(jnp.int32, (sub, 1, pst), 0)
        if reverse:
            a, b = nhi - 1 - a, sub - 1 - b
        hi_re, hi_im = closed_form((a * sub).astype(F32))
        lo_re, lo_im = closed_form(b.astype(F32))
        p_re = hi_re[:, None] * lo_re[None] - hi_im[:, None] * lo_im[None]
        p_im = hi_re[:, None] * lo_im[None] + hi_im[:, None] * lo_re[None]
        return p_re.reshape(chunk, 1, pst), p_im.reshape(chunk, 1, pst)

    def times(m_re, m_im, p_re, p_im):
        o_re = m_re[None] * p_re - m_im[None] * p_im
        o_im = m_re[None] * p_im + m_im[None] * p_re
        return o_re.reshape(chunk * grp, pst), o_im.reshape(chunk * grp, pst)

    p0_re, p0_im = powers(False)
    p1_re = p0_re * a_re[None] - p0_im * a_im[None]
    p1_im = p0_re * a_im[None] + p0_im * a_re[None]
    ca_re, ca_im = times(c_re, c_im, p1_re, p1_im)
    car_ref[0] = ca_re.astype(car_ref.dtype)
    cai_ref[0] = ca_im.astype(cai_ref.dtype)
    pr_re, pr_im = powers(True)
    wb_re, wb_im = times(bb_re, bb_im, pr_re, pr_im)
    wbr_ref[0] = wb_re.astype(wbr_ref.dtype)
    wbi_ref[0] = wb_im.astype(wbi_ref.dtype)
    lf = float(chunk)
    magl = jnp.exp(x_re * lf)
    al_ref[0] = jnp.concatenate([magl * jnp.cos(x_im * lf), magl * jnp.sin(x_im * lf)], axis=0)
    ce_re, ce_im = times(c_re, c_im, p0_re, p0_im)
    r_ref[...] = _dot_nt(bb_re, ce_re, HI) - _dot_nt(bb_im, ce_im, HI)
    width = chunk * grp
    lane = lax.broadcasted_iota(jnp.int32, (grp, width), 1)
    for j in range(chunk):
        r = r_ref[...]
        if j:
            r = jnp.where(lane >= j * grp, pltpu.roll(r, j * grp, axis=1), 0.0)
        tt_ref[0, j * grp:(j + 1) * grp, :] = r.astype(tt_ref.dtype)


def _s5_build(lam_re, lam_im, log_dt, b_re, b_im, c_re, c_im, chunk):
    ng, pst = lam_re.shape
    grp = S5_GROUP
    width = chunk * grp
    row = lambda x: x.astype(F32).reshape(ng, 1, pst)
    ldt = jnp.broadcast_to(log_dt.astype(F32)[:, None, None], (ng, 1, pst))
    args = (row(lam_re), row(lam_im), ldt, c_re.astype(F32), c_im.astype(F32),
            jnp.swapaxes(b_re.astype(F32), 1, 2), jnp.swapaxes(b_im.astype(F32), 1, 2))
    spec_row = pl.BlockSpec((1, 1, pst), lambda g: (g, 0, 0))
    spec_gp = pl.BlockSpec((1, grp, pst), lambda g: (g, 0, 0))
    spec_w = pl.BlockSpec((1, width, pst), lambda g: (g, 0, 0))
    return pl.pallas_call(
        functools.partial(_s5_build_body, chunk=chunk), grid=(ng,),
        in_specs=[spec_row, spec_row, spec_row, spec_gp, spec_gp, spec_gp, spec_gp],
        out_specs=[pl.BlockSpec((1, width, width), lambda g: (g, 0, 0)),
                   spec_w, spec_w, spec_w, spec_w,
                   pl.BlockSpec((1, 2, pst), lambda g: (g, 0, 0))],
        out_shape=[jax.ShapeDtypeStruct((ng, width, width), BF16),
                   jax.ShapeDtypeStruct((ng, width, pst), BF16),
                   jax.ShapeDtypeStruct((ng, width, pst), BF16),
                   jax.ShapeDtypeStruct((ng, width, pst), BF16),
                   jax.ShapeDtypeStruct((ng, width, pst), BF16),
                   jax.ShapeDtypeStruct((ng, 2, pst), F32)],
        scratch_shapes=[pltpu.VMEM((grp, width), F32)],
        compiler_params=_params("parallel"), name="s5_build",
    )(*args)


def _s5_scan_body(u_ref, tt_ref, wbr_ref, wbi_ref, car_ref, cai_ref, al_ref, y_ref,
                  sre_ref, sim_ref, *, bsz, nc):
    u = u_ref[0]
    sre_ref[...] = _dot(u, wbr_ref[0])
    sim_ref[...] = _dot(u, wbi_ref[0])
    a_re, a_im = al_ref[0, 0:1, :], al_ref[0, 1:2, :]

    def step(c, carry):
        out = []
        for b in range(bsz):
            h_re, h_im = carry[b]
            r = b * nc + c
            loc_re, loc_im = sre_ref[pl.ds(r, 1), :], sim_ref[pl.ds(r, 1), :]
            sre_ref[pl.ds(r, 1), :] = h_re
            sim_ref[pl.ds(r, 1), :] = h_im
            out.append((a_re * h_re - a_im * h_im + loc_re, a_re * h_im + a_im * h_re + loc_im))
        return tuple(out)

    zero = jnp.zeros((1, S5_STATE), F32)
    lax.fori_loop(0, nc, step, tuple((zero, zero) for _ in range(bsz)))
    y = _dot(u, tt_ref[0])
    y = y + _dot_nt(sre_ref[...].astype(BF16), car_ref[0])
    y = y - _dot_nt(sim_ref[...].astype(BF16), cai_ref[0])
    y_ref[0] = y.astype(y_ref.dtype)


def _s5_scan(u_g, tables, bsz):
    tt, wbr, wbi, car, cai, al = tables
    ng, rows, width = u_g.shape
    pst = S5_STATE
    nc = rows // bsz
    spec_w = pl.BlockSpec((1, width, pst), lambda g: (g, 0, 0))
    return pl.pallas_call(
        functools.partial(_s5_scan_body, bsz=bsz, nc=nc), grid=(ng,),
        in_specs=[pl.BlockSpec((1, rows, width), lambda g: (g, 0, 0)),
                  pl.BlockSpec((1, width, width), lambda g: (g, 0, 0)),
                  spec_w, spec_w, spec_w, spec_w,
                  pl.BlockSpec((1, 2, pst), lambda g: (g, 0, 0))],
        out_specs=pl.BlockSpec((1, rows, width), lambda g: (g, 0, 0)),
        out_shape=jax.ShapeDtypeStruct((ng, rows, width), BF16),
        scratch_shapes=[pltpu.VMEM((rows, pst), F32), pltpu.VMEM((rows, pst), F32)],
        compiler_params=_params("parallel"), name="s5_scan",
    )(u_g, tt, wbr, wbi, car, cai, al)


def _s5_act_body(y_ref, u_ref, d_ref, o_ref):
    o_ref[...] = jax.nn.gelu(y_ref[...].astype(F32) + d_ref[...] * u_ref[...]).astype(o_ref.dtype)


def _s5_act(y, u, d_skip, tm=512):
    m, w = y.shape
    tm = min(tm, m)
    spec = pl.BlockSpec((tm, w), lambda i: (i, 0))
    return pl.pallas_call(
        _s5_act_body, grid=(m // tm,),
        in_specs=[spec, spec, pl.BlockSpec((1, w), lambda i: (0, 0))],
        out_specs=spec, out_shape=jax.ShapeDtypeStruct((m, w), BF16),
        compiler_params=_params("parallel"), name="s5_act",
    )(y, u, d_skip.reshape(1, w).astype(F32))


def _s5_mixer(u, lam_re, lam_im, log_dt, b_re, b_im, c_re, c_im, d_skip, w_glu, layer, b_glu, bsz):
    m, width = u.shape
    seq = m // bsz
    ng = width // S5_GROUP
    chunk = min(S5_CHUNK, seq)
    nc = seq // chunk
    tables = _s5_build(lam_re, lam_im, log_dt, b_re, b_im, c_re, c_im, chunk)
    u_g = u.reshape(bsz * nc, chunk, ng, S5_GROUP).transpose(2, 0, 1, 3)
    u_g = u_g.reshape(ng, bsz * nc, chunk * S5_GROUP).astype(BF16)
    y_g = _s5_scan(u_g, tables, bsz)
    y = y_g.reshape(ng, bsz * nc, chunk, S5_GROUP).transpose(1, 2, 0, 3).reshape(m, width)
    z = _s5_act(y, u, d_skip)
    return _matmul(z, w_glu, layer=layer, out_dtype=BF16, act="glu", z=z, bias=b_glu,
                   tn=min(width, 768), tk=width)


def _gdn_conv_body(x_ref, halo_ref, w_ref, o_ref, buf_ref, *, ts, nh):
    i = pl.program_id(1)
    j = pl.program_id(2)
    halo = halo_ref[0]
    buf_ref[0:8, :] = jnp.where(i == 0, jnp.zeros_like(halo), halo)
    buf_ref[8:8 + ts, :] = x_ref[0]
    acc = None
    for tap in range(GDN_CONV):
        off = 8 - (GDN_CONV - 1) + tap
        term = buf_ref[off:off + ts, :] * w_ref[tap:tap + 1, :]
        acc = term if acc is None else acc + term
    y = acc * _sigmoid(acc)
    qscale = jnp.where(j == 0, HEAD_DIM ** -0.5, 1.0).astype(F32)
    for h in range(nh):
        sl = slice(h * HEAD_DIM, (h + 1) * HEAD_DIM)
        yh = y[:, sl]
        nrm = yh * lax.rsqrt(jnp.sum(yh * yh, axis=-1, keepdims=True) + EPS) * qscale
        o_ref[0, :, sl] = jnp.where(j == 2, yh, nrm)


def _gdn_conv(x, conv_w, ts=256):
    bsz, seq, w4 = x.shape
    wd = w4 // 4
    w3 = 3 * wd
    ts = min(ts, seq)
    per8 = ts // 8
    return pl.pallas_call(
        functools.partial(_gdn_conv_body, ts=ts, nh=wd // HEAD_DIM),
        grid=(bsz, seq // ts, 3),
        in_specs=[pl.BlockSpec((1, ts, wd), lambda b, i, j: (b, i, j)),
                  pl.BlockSpec((1, 8, wd), lambda b, i, j: (b, jnp.maximum(i * per8 - 1, 0), j)),
                  pl.BlockSpec((GDN_CONV, wd), lambda b, i, j: (0, j))],
        out_specs=pl.BlockSpec((1, ts, wd), lambda b, i, j: (b, i, j)),
        out_shape=jax.ShapeDtypeStruct((bsz, seq, w3), F32),
        scratch_shapes=[pltpu.VMEM((ts + 8, wd), F32)],
        compiler_params=_params("parallel", "parallel", "parallel"), name="gdn_conv",
    )(x, x, conv_w.astype(F32))


_G_GC, _G_EGC, _G_EDEC, _G_BETA, _G_GLAST = 0, 16, 32, 48, 64


def _gdn_gates_body(x_ref, alog_ref, dtb_ref, o_ref, *, ts):
    x = x_ref[...]
    g = -jnp.exp(alog_ref[...]) * _softplus(x + dtb_ref[...])
    r = lax.broadcasted_iota(jnp.int32, (ts, ts), 0)
    c = lax.broadcasted_iota(jnp.int32, (ts, ts), 1)
    same = (r // GDN_CHUNK) == (c // GDN_CHUNK)
    gc = _dot(jnp.where(same & (r >= c), 1.0, 0.0).astype(F32), g, HI)
    gl = _dot(jnp.where(same, 1.0, 0.0).astype(F32), g, HI)
    lane = lax.broadcasted_iota(jnp.int32, x.shape, 1)
    out = jnp.where(lane < _G_EGC, gc,
          jnp.where(lane < _G_EDEC, jnp.exp(gc),
          jnp.where(lane < _G_BETA, jnp.exp(gl - gc),
          jnp.where(lane < _G_GLAST, _sigmoid(x), jnp.exp(gl)))))
    o_ref[...] = out


def _gdn_gates(ab, a_log, dt_bias, ts=512):
    m = ab.shape[0]
    ts = min(ts, m)
    nh = a_log.shape[0]

    def lanes(p):
        row = jnp.zeros((LANES,), F32)
        for off in (_G_GC, _G_EGC, _G_EDEC, _G_GLAST):
            row = row.at[off:off + nh].set(p.astype(F32))
        return row.reshape(1, LANES)

    spec = pl.BlockSpec((ts, LANES), lambda i: (i, 0))
    prm = pl.BlockSpec((1, LANES), lambda i: (0, 0))
    return pl.pallas_call(
        functools.partial(_gdn_gates_body, ts=ts), grid=(m // ts,),
        in_specs=[spec, prm, prm], out_specs=spec,
        out_shape=jax.ShapeDtypeStruct((m, LANES), F32),
        compiler_params=_params("parallel"), name="gdn_gates",
    )(ab, lanes(a_log), lanes(dt_bias))


def _split3(x):
    hi = x.astype(BF16)
    lo = (x - hi.astype(F32)).astype(BF16)
    return hi, lo


def _dot3(a, b):
    a_hi, a_lo = _split3(a)
    b_hi, b_lo = _split3(b)
    lhs = jnp.concatenate([a_hi, a_lo, a_hi], axis=1)
    rhs = jnp.concatenate([b_hi, b_hi, b_lo], axis=0)
    return _dot(lhs, rhs)


def _gdn_intra_body(q_ref, k_ref, v_ref, gp_ref, gct_ref, u_ref, w_ref, qd_ref, kd_ref, at_ref,
                    *, ngrp, grows):
    cl = GDN_CHUNK
    h = pl.program_id(1)
    gp = gp_ref[0]
    gc_col = _lane_pick(gp, h + _G_GC)
    egc_col = _lane_pick(gp, h + _G_EGC)
    edec_col = _lane_pick(gp, h + _G_EDEC)
    beta_col = _lane_pick(gp, h + _G_BETA)
    r = lax.broadcasted_iota(jnp.int32, (grows, grows), 0)
    c = lax.broadcasted_iota(jnp.int32, (grows, grows), 1)
    same = (r // cl) == (c // cl)
    lower = same & (r >= c)
    strict = same & (r > c)
    steps = int(math.log2(cl))
    ps, sols = [], []
    for g in range(ngrp):
        rows = slice(g * grows, (g + 1) * grows)
        q, k, v = q_ref[0, rows, :], k_ref[0, rows, :], v_ref[0, rows, :]
        beta, egc = beta_col[rows], egc_col[rows]
        gc_row = gct_ref[0, 0, :, rows]
        decay = jnp.exp(jnp.where(lower, gc_col[rows] - gc_row, -jnp.inf))
        kb, vb = k * beta, v * beta
        kbf = k.astype(BF16)
        ps.append(-jnp.where(strict, _dot_nt(kb.astype(BF16), kbf) * decay, 0.0))
        sols.append(jnp.concatenate([vb, kb * egc], axis=-1))
        qd_ref[0, rows, :] = (q * egc).astype(qd_ref.dtype)
        kd_ref[0, rows, :] = (k * edec_col[rows]).astype(kd_ref.dtype)
        attn = jnp.where(lower, _dot_nt(q.astype(BF16), kbf) * decay, 0.0)
        for n in range(grows // cl):
            at_ref[0, 0, g * grows + n * cl:g * grows + (n + 1) * cl, :] = (
                attn[n * cl:(n + 1) * cl, n * cl:(n + 1) * cl].astype(at_ref.dtype))
    for s in range(steps):
        for g in range(ngrp):
            sols[g] = sols[g] + _dot3(ps[g], sols[g])
            if s + 1 < steps:
                ps[g] = _dot3(ps[g], ps[g])
    for g in range(ngrp):
        rows = slice(g * grows, (g + 1) * grows)
        u_ref[0, rows, :] = sols[g][:, :HEAD_DIM]
        w_ref[0, rows, :] = sols[g][:, HEAD_DIM:].astype(w_ref.dtype)


def _gdn_intra(qkv, gates, gct, ngrp=4, grows=256):
    bsz, seq, w3 = qkv.shape
    wd = w3 // 3
    nh = wd // HEAD_DIM
    cl = GDN_CHUNK
    grows = min(grows, seq)
    ngrp = min(ngrp, seq // grows)
    tt = ngrp * grows
    col = lambda off: pl.BlockSpec((1, tt, HEAD_DIM), lambda b, h, i: (b, i, h + off))
    tok_sd = lambda dt: jax.ShapeDtypeStruct((bsz, seq, wd), dt)
    return pl.pallas_call(
        functools.partial(_gdn_intra_body, ngrp=ngrp, grows=grows), grid=(bsz, nh, seq // tt),
        in_specs=[col(0), col(nh), col(2 * nh),
                  pl.BlockSpec((1, tt, LANES), lambda b, h, i: (b, i, 0)),
                  pl.BlockSpec((1, 1, 1, tt), lambda b, h, i: (b, h, 0, i))],
        out_specs=[col(0), col(0), col(0), col(0),
                   pl.BlockSpec((1, 1, tt, cl), lambda b, h, i: (b, h, i, 0))],
        out_shape=[tok_sd(F32), tok_sd(BF16), tok_sd(BF16), tok_sd(BF16),
                   jax.ShapeDtypeStruct((bsz, nh, seq, cl), BF16)],
        compiler_params=_params("parallel", "parallel", "parallel"), name="gdn_intra",
    )(qkv, qkv, qkv, gates, gct)


def _gdn_scan_body(gl_ref, u_ref, w_ref, qd_ref, kd_ref, at_ref, gate_ref, on_ref, o_ref,
                   state_ref, *, nb, nh):
    cl = GDN_CHUNK
    b = pl.program_id(0)
    i = pl.program_id(1)

    @pl.when(i == 0)
    def _():
        state_ref[...] = jnp.zeros_like(state_ref)

    for n in range(nb):
        rows = slice(n * cl, (n + 1) * cl)
        from_state = []
        for h in range(nh):
            sl = slice(h * HEAD_DIM, (h + 1) * HEAD_DIM)
            wq = jnp.concatenate([w_ref[0, rows, sl], qd_ref[0, rows, sl]], axis=0).astype(BF16)
            from_state.append(_dot(wq, state_ref[h].astype(BF16)))
        for h in range(nh):
            sl = slice(h * HEAD_DIM, (h + 1) * HEAD_DIM)
            v_new = u_ref[0, rows, sl] - from_state[h][:cl]
            vb = v_new.astype(BF16)
            out = from_state[h][cl:] + _dot(at_ref[0, h, rows, :].astype(BF16), vb)
            g_last = gl_ref[b, i * nb + n, h]
            state_ref[h] = state_ref[h] * g_last + _dot_tn(kd_ref[0, rows, sl].astype(BF16), vb)
            gate = gate_ref[0, rows, sl]
            nrm = out * lax.rsqrt(jnp.mean(out * out, axis=-1, keepdims=True) + EPS) * on_ref[...]
            o_ref[0, rows, sl] = (nrm * (gate * _sigmoid(gate))).astype(o_ref.dtype)


def _gdn_scan(glast, u_c, w_c, q_dec, k_dec, attn, main, o_norm, nb=4):
    bsz, seq, wd = u_c.shape
    nh = wd // HEAD_DIM
    cl = GDN_CHUNK
    nb = min(nb, seq // cl)
    tt = nb * cl
    tok = pl.BlockSpec((1, tt, wd), lambda b, i: (b, i, 0))
    return pl.pallas_call(
        functools.partial(_gdn_scan_body, nb=nb, nh=nh), grid=(bsz, seq // tt),
        in_specs=[pl.BlockSpec(memory_space=pltpu.SMEM), tok, tok, tok, tok,
                  pl.BlockSpec((1, nh, tt, cl), lambda b, i: (b, 0, i, 0)),
                  pl.BlockSpec((1, tt, wd), lambda b, i: (b, i, 3)),
                  pl.BlockSpec((1, HEAD_DIM), lambda b, i: (0, 0))],
        out_specs=tok,
        out_shape=jax.ShapeDtypeStruct((bsz, seq, wd), BF16),
        scratch_shapes=[pltpu.VMEM((nh, HEAD_DIM, HEAD_DIM), F32)],
        compiler_params=_params("parallel", "arbitrary"), name="gdn_scan",
    )(glast, u_c, w_c, q_dec, k_dec, attn, main, o_norm.reshape(1, HEAD_DIM).astype(F32))


def _gdn_mixer(main, ab, conv_w, a_log, dt_bias, o_norm):
    bsz, seq, w4 = main.shape
    nh = w4 // 4 // HEAD_DIM
    qkv = _gdn_conv(main, conv_w)
    gates = _gdn_gates(ab, a_log, dt_bias).reshape(bsz, seq, LANES)
    gct = gates[:, :, _G_GC:_G_GC + nh].transpose(0, 2, 1).reshape(bsz, nh, 1, seq)
    glast = gates[:, GDN_CHUNK - 1::GDN_CHUNK, _G_GLAST:_G_GLAST + nh]
    u_c, w_c, q_dec, k_dec, attn = _gdn_intra(qkv, gates, gct)
    return _gdn_scan(glast, u_c, w_c, q_dec, k_dec, attn, main, o_norm)


def _fox_cumf_body(x_ref, bf_ref, o_ref, carry_ref, *, ts):
    @pl.when(pl.program_id(1) == 0)
    def _():
        carry_ref[...] = jnp.zeros_like(carry_ref)

    ls = -_softplus(-(x_ref[0] + bf_ref[...]))
    r = lax.broadcasted_iota(jnp.int32, (ts, ts), 0)
    c = lax.broadcasted_iota(jnp.int32, (ts, ts), 1)
    cum = _dot(jnp.where(r >= c, 1.0, 0.0).astype(F32), ls, HI) + carry_ref[...]
    o_ref[0] = cum
    carry_ref[...] = cum[ts - 1:ts, :]


def _fox_cumf(f_logit, b_f, ts=256):
    bsz, seq, _ = f_logit.shape
    ts = min(ts, seq)
    bf = jnp.zeros((LANES,), F32).at[:b_f.shape[0]].set(b_f.astype(F32)).reshape(1, LANES)
    spec = pl.BlockSpec((1, ts, LANES), lambda b, i: (b, i, 0))
    return pl.pallas_call(
        functools.partial(_fox_cumf_body, ts=ts), grid=(bsz, seq // ts),
        in_specs=[spec, pl.BlockSpec((1, LANES), lambda b, i: (0, 0))],
        out_specs=spec, out_shape=jax.ShapeDtypeStruct((bsz, seq, LANES), F32),
        scratch_shapes=[pltpu.VMEM((1, LANES), F32)],
        compiler_params=_params("parallel", "arbitrary"), name="fox_cumf",
    )(f_logit, bf)


LOG2E = math.log2(math.e)
FOX_QSCALE = HEAD_DIM ** -0.5 * LOG2E


def _fox_attn_body(q_ref, k_ref, v_ref, cft_ref, o_ref, *scratch, tq, nblk):
    qi = pl.program_id(2)
    tb = tq // nblk
    m_refs, l_refs, acc_refs = scratch[0::3], scratch[1::3], scratch[2::3]
    for blk in range(nblk):
        m_refs[blk][...] = jnp.full_like(m_refs[blk], -jnp.inf)
        l_refs[blk][...] = jnp.zeros_like(l_refs[blk])
        acc_refs[blk][...] = jnp.zeros_like(acc_refs[blk])

    def scores(blk, kb):
        return _dot_nt(q_ref[0, blk * tb:(blk + 1) * tb, :], kb)

    def softmax_step(blk, s, ck, masked):
        m_ref, l_ref = m_refs[blk], l_refs[blk]
        s = s - ck
        if masked:
            r = lax.broadcasted_iota(jnp.int32, s.shape, 0) + blk * tb
            c = lax.broadcasted_iota(jnp.int32, s.shape, 1)
            s = jnp.where(r >= c, s, -jnp.inf)
        m_prev = m_ref[...]
        m_new = jnp.maximum(m_prev, jnp.max(s, axis=-1, keepdims=True))
        alpha = jnp.exp2(m_prev - m_new)
        p = jnp.exp2(s - m_new)
        l_ref[...] = alpha * l_ref[...] + jnp.sum(p, axis=-1, keepdims=True)
        m_ref[...] = m_new
        return p.astype(BF16), alpha

    def accumulate(blk, p, alpha, vb):
        acc_refs[blk][...] = alpha * acc_refs[blk][...] + _dot(p, vb)

    def body(j, carry):
        start = pl.multiple_of(j * tq, tq)
        kb, vb = k_ref[0, pl.ds(start, tq), :], v_ref[0, pl.ds(start, tq), :]
        ck = cft_ref[0, 0, j] * LOG2E
        s_all = [scores(blk, kb) for blk in range(nblk)]
        for blk in range(nblk):
            p, alpha = softmax_step(blk, s_all[blk], ck, False)
            accumulate(blk, p, alpha, vb)
        return carry

    lax.fori_loop(0, qi, body, 0)
    d0 = pl.multiple_of(qi * tq, tq)
    ck = cft_ref[0, 0, qi] * LOG2E
    nkeys = [(blk + 1) * tb for blk in range(nblk)]
    for blk in range(nblk):
        s_cur = scores(blk, k_ref[0, pl.ds(d0, nkeys[blk]), :])
        p, alpha = softmax_step(blk, s_cur, ck[:, :nkeys[blk]], True)
        accumulate(blk, p, alpha, v_ref[0, pl.ds(d0, nkeys[blk]), :])
        o_ref[0, blk * tb:(blk + 1) * tb, :] = (acc_refs[blk][...] / l_refs[blk][...]).astype(o_ref.dtype)


def _fox_attention(qkv, cumf_t, tq, nblk=4):
    bsz, seq, w3 = qkv.shape
    wd = w3 // 3
    nh = wd // HEAD_DIM
    nq = seq // tq
    tb = tq // nblk
    kv = lambda off: pl.BlockSpec((1, seq, HEAD_DIM), lambda b, h, qi: (b, 0, h + off))
    scratch = []
    for _ in range(nblk):
        scratch += [pltpu.VMEM((tb, 1), F32), pltpu.VMEM((tb, 1), F32), pltpu.VMEM((tb, HEAD_DIM), F32)]
    return pl.pallas_call(
        functools.partial(_fox_attn_body, tq=tq, nblk=nblk), grid=(bsz, nh, nq),
        in_specs=[pl.BlockSpec((1, tq, HEAD_DIM), lambda b, h, qi: (b, qi, h)),
                  kv(nh), kv(2 * nh),
                  pl.BlockSpec((1, 1, nq, 1, tq), lambda b, h, qi: (b, h, 0, 0, 0))],
        out_specs=pl.BlockSpec((1, tq, HEAD_DIM), lambda b, h, qi: (b, qi, h)),
        out_shape=jax.ShapeDtypeStruct((bsz, seq, wd), BF16),
        scratch_shapes=scratch,
        compiler_params=_params("parallel", "parallel", "arbitrary"),
        name="fox_attention",
    )(qkv, qkv, qkv, cumf_t)


def _fox_mixer(qkv, f_logit, b_f, tq=1024):
    bsz, seq, w3 = qkv.shape
    nh = w3 // 3 // HEAD_DIM
    tq = min(tq, seq)
    cumf = _fox_cumf(f_logit, b_f)
    cumf_t = cumf[:, :, :nh].transpose(0, 2, 1).reshape(bsz, nh, seq // tq, 1, tq)
    return _fox_attention(qkv, cumf_t, tq)


def _pad_cols(w, groups):
    out = jnp.zeros((w.shape[0], LANES), w.dtype)
    for off in groups:
        out = out.at[:, off:off + w.shape[1]].set(w)
    return out


def kernel(x, mem, mem_norm, w_mem_kv, norm1, w_out, norm2, w_up, w_down, norm_f,
           s5_w_in, s5_lam_re, s5_lam_im, s5_log_dt, s5_b_re, s5_b_im, s5_c_re, s5_c_im,
           s5_d_skip, s5_w_glu, s5_b_glu,
           gdn_w_in, gdn_conv_w, gdn_a_log, gdn_dt_bias, gdn_o_norm,
           fox_w_in, fox_b_f):
    bsz, seq, d = x.shape
    m = bsz * seq
    depth = norm1.shape[0]
    wd = d - MEM_WIDTH
    nh = wd // HEAD_DIM
    mlen = mem.shape[1]

    mem_a = _rmsnorm(mem.reshape(bsz * mlen, d), mem_norm)
    mkv = _matmul(mem_a, w_mem_kv.astype(BF16)[None], out_dtype=BF16, tn=512)
    mem_k = mkv[:, :MEM_WIDTH].reshape(bsz, mlen, MEM_WIDTH)
    mem_v = mkv[:, MEM_WIDTH:].reshape(bsz, mlen, MEM_WIDTH)

    w_out_b, w_up_b, w_down_b = w_out.astype(BF16), w_up.astype(BF16), w_down.astype(BF16)
    s5_w_in_b, s5_w_glu_b = s5_w_in.astype(BF16), s5_w_glu.astype(BF16)
    gdn_w_in_b, fox_w_in_b = gdn_w_in.astype(BF16), fox_w_in.astype(BF16)

    h = x.reshape(m, d)
    a = _rmsnorm(h, norm1[0])
    out = None
    for i in range(depth):
        kind, j = i % 3, i // 3
        if kind == 0:
            u = _matmul(a, s5_w_in_b, layer=j, n=wd, out_dtype=F32, tn=768)
            mq = _matmul(a, s5_w_in_b, layer=j, col0=wd // MEM_WIDTH, n=MEM_WIDTH, out_dtype=BF16,
                         tn=MEM_WIDTH)
            mix = _s5_mixer(u, s5_lam_re[j], s5_lam_im[j], s5_log_dt[j], s5_b_re[j], s5_b_im[j],
                            s5_c_re[j], s5_c_im[j], s5_d_skip[j], s5_w_glu_b, j, s5_b_glu[j], bsz)
        elif kind == 1:
            w_in = gdn_w_in[j]
            w_ab = (_pad_cols(w_in[:, 4 * wd:4 * wd + nh], (_G_GC, _G_EGC, _G_EDEC, _G_GLAST))
                    + _pad_cols(w_in[:, 4 * wd + nh:4 * wd + 2 * nh], (_G_BETA,)))
            main = _matmul(a, gdn_w_in_b, layer=j, n=4 * wd, out_dtype=F32, tn=768)
            ab = _matmul(a, w_ab.astype(BF16)[None], out_dtype=F32)
            mq = _matmul(a, w_in[:, -MEM_WIDTH:].astype(BF16)[None], out_dtype=BF16, tn=MEM_WIDTH)
            main = main.reshape(bsz, seq, 4 * wd)
            mix = _gdn_mixer(main, ab, gdn_conv_w[j], gdn_a_log[j], gdn_dt_bias[j],
                             gdn_o_norm[j]).reshape(m, wd)
        else:
            w_in = fox_w_in[j]
            qscale = jnp.concatenate([jnp.full((wd,), FOX_QSCALE, F32), jnp.ones((2 * wd,), F32)])
            qkv = _matmul(a, fox_w_in_b, layer=j, n=3 * wd, out_dtype=BF16, act="scale",
                          bias=qscale, tn=768)
            w_f = _pad_cols(w_in[:, 3 * wd:3 * wd + nh], (0,))
            fl = _matmul(a, w_f.astype(BF16)[None], out_dtype=F32)
            mq = _matmul(a, w_in[:, -MEM_WIDTH:].astype(BF16)[None], out_dtype=BF16, tn=MEM_WIDTH)
            mix = _fox_mixer(qkv.reshape(bsz, seq, 3 * wd), fl.reshape(bsz, seq, LANES),
                             fox_b_f[j]).reshape(m, wd)
        read = _mem_attention(mq.reshape(bsz, seq, MEM_WIDTH), mem_k, mem_v).reshape(m, MEM_WIDTH)
        h, a = _matmul_cat_norm(mix, read, w_out_b, i, h, norm2[i])
        up = _matmul(a, w_up_b, layer=i, out_dtype=BF16, act="relu2")
        if i + 1 < depth:
            h, a = _matmul_res_norm(up, w_down_b, i, h, norm1[i + 1], norm_dtype=BF16, keep_h=True)
        else:
            _, out = _matmul_res_norm(up, w_down_b, i, h, norm_f, norm_dtype=x.dtype, keep_h=False)
    return out.reshape(bsz, seq, d)
```

```python
import functools
import math

import jax
import jax.numpy as jnp
from jax import lax
from jax.experimental import pallas as pl
from jax.experimental.pallas import tpu as pltpu

F32 = jnp.float32
BF16 = jnp.bfloat16
EPS = 1e-6
HEAD_DIM = 128
LANES = 128
MEM_HEADS = 4
MEM_WIDTH = MEM_HEADS * HEAD_DIM
S5_GROUP = 16
S5_STATE = 64
S5_CHUNK = 64
GDN_CHUNK = 64
GDN_CONV = 4
VMEM_LIMIT_BYTES = 48 * 1024 * 1024
HI = lax.Precision.HIGHEST


def _params(*sem):
    return pltpu.CompilerParams(dimension_semantics=sem, vmem_limit_bytes=VMEM_LIMIT_BYTES)


def _dot(a, b, precision=None):
    return jnp.dot(a, b, preferred_element_type=F32, precision=precision)


def _dot_nt(a, b, precision=None):
    return lax.dot_general(a, b, (((1,), (1,)), ((), ())), preferred_element_type=F32,
                           precision=precision)


def _dot_tn(a, b, precision=None):
    return lax.dot_general(a, b, (((0,), (0,)), ((), ())), preferred_element_type=F32,
                           precision=precision)


def _softplus(x):
    return jnp.maximum(x, 0.0) + jnp.log1p(jnp.exp(-jnp.abs(x)))


def _sigmoid(x):
    return 1.0 / (1.0 + jnp.exp(-x))


def _lane_pick(x, lane):
    ids = lax.broadcasted_iota(jnp.int32, x.shape, 1)
    return jnp.sum(jnp.where(ids == lane, x, 0.0), axis=-1, keepdims=True)


def _rmsnorm_body(x_ref, g_ref, o_ref):
    x = x_ref[...].astype(F32)
    ms = jnp.mean(x * x, axis=-1, keepdims=True)
    o_ref[...] = (x * lax.rsqrt(ms + EPS) * g_ref[...]).astype(o_ref.dtype)


def _rmsnorm(x2d, gain, out_dtype=BF16, tm=512):
    m, d = x2d.shape
    tm = min(tm, m)
    return pl.pallas_call(
        _rmsnorm_body, grid=(m // tm,),
        in_specs=[pl.BlockSpec((tm, d), lambda i: (i, 0)),
                  pl.BlockSpec((1, d), lambda i: (0, 0))],
        out_specs=pl.BlockSpec((tm, d), lambda i: (i, 0)),
        out_shape=jax.ShapeDtypeStruct((m, d), out_dtype),
        compiler_params=_params("parallel"), name="rmsnorm",
    )(x2d, gain.reshape(1, d).astype(F32))


def _mm_body(*refs, nk, act, has_res):
    a_ref, w_ref = refs[0], refs[1]
    pos = 2
    z_ref = b_ref = res_ref = None
    if act == "glu":
        z_ref = refs[pos]
        pos += 1
    if act in ("glu", "scale"):
        b_ref = refs[pos]
        pos += 1
    if has_res:
        res_ref = refs[pos]
        pos += 1
    o_ref = refs[pos]
    acc_ref = refs[pos + 1] if nk > 1 else None

    def epilogue(y):
        if act == "relu2":
            y = jnp.square(jnp.maximum(y, 0.0))
        elif act == "glu":
            y = z_ref[...].astype(F32) * _sigmoid(y + b_ref[...])
        elif act == "scale":
            y = y * b_ref[...]
        if has_res:
            y = y + res_ref[...]
        o_ref[...] = y.astype(o_ref.dtype)

    part = _dot(a_ref[...], w_ref[...])
    if nk == 1:
        epilogue(part)
        return
    k = pl.program_id(2)

    @pl.when(k == 0)
    def _():
        acc_ref[...] = part

    @pl.when(k > 0)
    def _():
        acc_ref[...] += part

    @pl.when(k == nk - 1)
    def _():
        epilogue(acc_ref[...])


def _matmul(a, w, *, out_dtype, layer=0, col0=0, n=None, act=None, z=None, bias=None, res=None,
            tm=1024, tn=1024, tk=2048):
    m, kdim = a.shape
    n = w.shape[2] if n is None else n
    tm, tn, tk = min(tm, m), min(tn, n), min(tk, kdim)
    assert m % tm == 0 and n % tn == 0 and kdim % tk == 0, (a.shape, w.shape, tm, tn, tk)
    nk = kdim // tk
    in_specs = [pl.BlockSpec((tm, tk), lambda i, j, k: (i, k)),
                pl.BlockSpec((None, tk, tn), lambda i, j, k: (layer, k, j + col0))]
    args = [a, w]
    if act == "glu":
        in_specs.append(pl.BlockSpec((tm, tn), lambda i, j, k: (i, j)))
        args.append(z)
    if act in ("glu", "scale"):
        in_specs.append(pl.BlockSpec((1, tn), lambda i, j, k: (0, j)))
        args.append(bias.reshape(1, n).astype(F32))
    if res is not None:
        in_specs.append(pl.BlockSpec((tm, tn), lambda i, j, k: (i, j)))
        args.append(res)
    scratch = [pltpu.VMEM((tm, tn), F32)] if nk > 1 else []
    return pl.pallas_call(
        functools.partial(_mm_body, nk=nk, act=act, has_res=res is not None),
        grid=(m // tm, n // tn, nk),
        in_specs=in_specs,
        out_specs=pl.BlockSpec((tm, tn), lambda i, j, k: (i, j)),
        out_shape=jax.ShapeDtypeStruct((m, n), out_dtype),
        scratch_shapes=scratch,
        compiler_params=_params("parallel", "parallel", "arbitrary"), name="matmul",
    )(*args)


def _norm_rows(y, gain):
    return y * lax.rsqrt(jnp.mean(y * y, axis=-1, keepdims=True) + EPS) * gain


def _mm_res_norm_body(a_ref, w_ref, res_ref, g_ref, *refs, nk, keep_h):
    acc_ref, norm_ref = refs if keep_h else refs[::-1]
    k = pl.program_id(1)

    @pl.when(k == 0)
    def _():
        acc_ref[...] = res_ref[...] + _dot(a_ref[...], w_ref[...])

    @pl.when(k > 0)
    def _():
        acc_ref[...] += _dot(a_ref[...], w_ref[...])

    @pl.when(k == nk - 1)
    def _():
        norm_ref[...] = _norm_rows(acc_ref[...], g_ref[...]).astype(norm_ref.dtype)


def _matmul_res_norm(a, w, layer, res, gain, *, norm_dtype, keep_h, tm=512, tk=2048):
    m, kdim = a.shape
    n = w.shape[2]
    tm, tk = min(tm, m), min(tk, kdim)
    assert m % tm == 0 and kdim % tk == 0
    nk = kdim // tk
    row = pl.BlockSpec((tm, n), lambda i, k: (i, 0))
    out_specs = [row] * (2 if keep_h else 1)
    out_shape = ([jax.ShapeDtypeStruct((m, n), F32)] if keep_h else []) + [
        jax.ShapeDtypeStruct((m, n), norm_dtype)]
    outs = pl.pallas_call(
        functools.partial(_mm_res_norm_body, nk=nk, keep_h=keep_h), grid=(m // tm, nk),
        in_specs=[pl.BlockSpec((tm, tk), lambda i, k: (i, k)),
                  pl.BlockSpec((None, tk, n), lambda i, k: (layer, k, 0)),
                  row, pl.BlockSpec((1, n), lambda i, k: (0, 0))],
        out_specs=out_specs, out_shape=out_shape,
        scratch_shapes=[] if keep_h else [pltpu.VMEM((tm, n), F32)],
        compiler_params=_params("parallel", "arbitrary"), name="matmul_res_norm",
    )(a, w, res, gain.reshape(1, n).astype(F32))
    return outs if keep_h else (None, outs[0])


def _mm_cat_norm_body(a1_ref, a2_ref, w1_ref, w2_ref, res_ref, g_ref, h_ref, n_ref):
    y = _dot(a1_ref[...], w1_ref[...]) + _dot(a2_ref[...], w2_ref[...]) + res_ref[...]
    h_ref[...] = y
    n_ref[...] = _norm_rows(y, g_ref[...]).astype(n_ref.dtype)


def _matmul_cat_norm(a1, a2, w, layer, res, gain, tm=512):
    m, k1 = a1.shape
    k2 = a2.shape[1]
    n = w.shape[2]
    tm = min(tm, m)
    assert m % tm == 0 and w.shape[1] == k1 + k2 and k1 % k2 == 0
    row = pl.BlockSpec((tm, n), lambda i: (i, 0))
    return pl.pallas_call(
        _mm_cat_norm_body, grid=(m // tm,),
        in_specs=[pl.BlockSpec((tm, k1), lambda i: (i, 0)),
                  pl.BlockSpec((tm, k2), lambda i: (i, 0)),
                  pl.BlockSpec((None, k1, n), lambda i: (layer, 0, 0)),
                  pl.BlockSpec((None, k2, n), lambda i: (layer, k1 // k2, 0)),
                  row, pl.BlockSpec((1, n), lambda i: (0, 0))],
        out_specs=[row, row],
        out_shape=[jax.ShapeDtypeStruct((m, n), F32), jax.ShapeDtypeStruct((m, n), BF16)],
        compiler_params=_params("parallel"), name="matmul_cat_norm",
    )(a1, a2, w, w, res, gain.reshape(1, n).astype(F32))


def _mem_attn_body(q_ref, k_ref, v_ref, o_ref):
    scale = HEAD_DIM ** -0.5
    for h in range(MEM_HEADS):
        sl = slice(h * HEAD_DIM, (h + 1) * HEAD_DIM)
        logits = _dot_nt(q_ref[0, :, sl], k_ref[0, :, sl]) * scale
        mx = jnp.max(logits, axis=-1, keepdims=True)
        e = jnp.exp(logits - mx)
        p = e / jnp.sum(e, axis=-1, keepdims=True)
        o_ref[0, :, sl] = _dot(p.astype(BF16), v_ref[0, :, sl]).astype(o_ref.dtype)


def _mem_attention(q, mem_k, mem_v, ts=1024):
    bsz, seq, _ = q.shape
    mlen = mem_k.shape[1]
    ts = min(ts, seq)
    return pl.pallas_call(
        _mem_attn_body, grid=(bsz, seq // ts),
        in_specs=[pl.BlockSpec((1, ts, MEM_WIDTH), lambda b, i: (b, i, 0)),
                  pl.BlockSpec((1, mlen, MEM_WIDTH), lambda b, i: (b, 0, 0)),
                  pl.BlockSpec((1, mlen, MEM_WIDTH), lambda b, i: (b, 0, 0))],
        out_specs=pl.BlockSpec((1, ts, MEM_WIDTH), lambda b, i: (b, i, 0)),
        out_shape=jax.ShapeDtypeStruct((bsz, seq, MEM_WIDTH), BF16),
        compiler_params=_params("parallel", "parallel"), name="mem_attention",
    )(q, mem_k, mem_v)


def _s5_build_body(lre_ref, lim_ref, ldt_ref, cre_ref, cim_ref, btr_ref, bti_ref,
                   tt_ref, wbr_ref, wbi_ref, car_ref, cai_ref, al_ref, r_ref, *, chunk):
    grp, pst = S5_GROUP, S5_STATE
    l_re, l_im = lre_ref[0], lim_ref[0]
    dt = jnp.exp(ldt_ref[0])
    x_re, x_im = l_re * dt, l_im * dt
    a_mag = jnp.exp(x_re)
    a_re, a_im = a_mag * jnp.cos(x_im), a_mag * jnp.sin(x_im)
    den = l_re * l_re + l_im * l_im
    z_re = ((a_re - 1.0) * l_re + a_im * l_im) / den
    z_im = (a_im * l_re - (a_re - 1.0) * l_im) / den
    bt_re, bt_im = btr_ref[0], bti_ref[0]
    bb_re = z_re * bt_re - z_im * bt_im
    bb_im = z_re * bt_im + z_im * bt_re
    c_re, c_im = cre_ref[0], cim_ref[0]

    sub = 8
    nhi = chunk // sub

    def closed_form(t):
        mag = jnp.exp(x_re[None] * t)
        ang = x_im[None] * t
        return mag * jnp.cos(ang), mag * jnp.sin(ang)

    def powers(reverse):
        a = lax.broadcasted_iota(jnp.int32, (nhi, 1, pst), 0)
        b = lax.broadcasted_iota(jnp.int32, (sub, 1, pst), 0)
        if reverse:
            a, b = nhi - 1 - a, sub - 1 - b
        hi_re, hi_im = closed_form((a * sub).astype(F32))
        lo_re, lo_im = closed_form(b.astype(F32))
        p_re = hi_re[:, None] * lo_re[None] - hi_im[:, None] * lo_im[None]
        p_im = hi_re[:, None] * lo_im[None] + hi_im[:, None] * lo_re[None]
        return p_re.reshape(chunk, 1, pst), p_im.reshape(chunk, 1, pst)

    def times(m_re, m_im, p_re, p_im):
        o_re = m_re[None] * p_re - m_im[None] * p_im
        o_im = m_re[None] * p_im + m_im[None] * p_re
        return o_re.reshape(chunk * grp, pst), o_im.reshape(chunk * grp, pst)

    p0_re, p0_im = powers(False)
    p1_re = p0_re * a_re[None] - p0_im * a_im[None]
    p1_im = p0_re * a_im[None] + p0_im * a_re[None]
    ca_re, ca_im = times(c_re, c_im, p1_re, p1_im)
    car_ref[0] = ca_re.astype(car_ref.dtype)
    cai_ref[0] = ca_im.astype(cai_ref.dtype)
    pr_re, pr_im = powers(True)
    wb_re, wb_im = times(bb_re, bb_im, pr_re, pr_im)
    wbr_ref[0] = wb_re.astype(wbr_ref.dtype)
    wbi_ref[0] = wb_im.astype(wbi_ref.dtype)
    lf = float(chunk)
    magl = jnp.exp(x_re * lf)
    al_ref[0] = jnp.concatenate([magl * jnp.cos(x_im * lf), magl * jnp.sin(x_im * lf)], axis=0)
    ce_re, ce_im = times(c_re, c_im, p0_re, p0_im)
    r_ref[...] = _dot_nt(bb_re, ce_re, HI) - _dot_nt(bb_im, ce_im, HI)
    width = chunk * grp
    lane = lax.broadcasted_iota(jnp.int32, (grp, width), 1)
    for j in range(chunk):
        r = r_ref[...]
        if j:
            r = jnp.where(lane >= j * grp, pltpu.roll(r, j * grp, axis=1), 0.0)
        tt_ref[0, j * grp:(j + 1) * grp, :] = r.astype(tt_ref.dtype)


def _s5_build(lam_re, lam_im, log_dt, b_re, b_im, c_re, c_im, chunk):
    ng, pst = lam_re.shape
    grp = S5_GROUP
    width = chunk * grp
    row = lambda x: x.astype(F32).reshape(ng, 1, pst)
    ldt = jnp.broadcast_to(log_dt.astype(F32)[:, None, None], (ng, 1, pst))
    args = (row(lam_re), row(lam_im), ldt, c_re.astype(F32), c_im.astype(F32),
            jnp.swapaxes(b_re.astype(F32), 1, 2), jnp.swapaxes(b_im.astype(F32), 1, 2))
    spec_row = pl.BlockSpec((1, 1, pst), lambda g: (g, 0, 0))
    spec_gp = pl.BlockSpec((1, grp, pst), lambda g: (g, 0, 0))
    spec_w = pl.BlockSpec((1, width, pst), lambda g: (g, 0, 0))
    return pl.pallas_call(
        functools.partial(_s5_build_body, chunk=chunk), grid=(ng,),
        in_specs=[spec_row, spec_row, spec_row, spec_gp, spec_gp, spec_gp, spec_gp],
        out_specs=[pl.BlockSpec((1, width, width), lambda g: (g, 0, 0)),
                   spec_w, spec_w, spec_w, spec_w,
                   pl.BlockSpec((1, 2, pst), lambda g: (g, 0, 0))],
        out_shape=[jax.ShapeDtypeStruct((ng, width, width), BF16),
                   jax.ShapeDtypeStruct((ng, width, pst), BF16),
                   jax.ShapeDtypeStruct((ng, width, pst), BF16),
                   jax.ShapeDtypeStruct((ng, width, pst), BF16),
                   jax.ShapeDtypeStruct((ng, width, pst), BF16),
                   jax.ShapeDtypeStruct((ng, 2, pst), F32)],
        scratch_shapes=[pltpu.VMEM((grp, width), F32)],
        compiler_params=_params("parallel"), name="s5_build",
    )(*args)


S5_OCT = LANES // S5_GROUP


def _s5_scan_body(x_ref, tt_ref, wbr_ref, wbi_ref, car_ref, cai_ref, al_ref, d_ref, z_ref,
                  acc_ref, xw_ref, ug_ref, sre_ref, sim_ref, *, bsz, nc, chunk):
    grp = S5_GROUP
    g = pl.program_id(1)
    rows = bsz * nc
    lane_grp = lax.broadcasted_iota(jnp.int32, (rows // 2, LANES), 1) // grp

    def shift(to_grp, from_grp):
        return lax.rem((to_grp - from_grp) * grp + LANES, LANES)

    def words(t):
        return pltpu.bitcast(t.astype(BF16), jnp.uint32)

    @pl.when(g == 0)
    def _():
        acc_ref[...] = jnp.zeros_like(acc_ref)
        for j in range(chunk):
            xw_ref[j] = words(x_ref[j])

    for col in range(chunk // S5_OCT):
        packed = jnp.zeros((rows // 2, LANES), jnp.uint32)
        for jj in range(S5_OCT):
            rolled = pltpu.roll(xw_ref[col * S5_OCT + jj], shift(jj, g), axis=1)
            packed = jnp.where(lane_grp == jj, rolled, packed)
        ug_ref[:, col * LANES:(col + 1) * LANES] = pltpu.bitcast(packed, BF16)

    u = ug_ref[...]
    sre_ref[...] = _dot(u, wbr_ref[0])
    sim_ref[...] = _dot(u, wbi_ref[0])
    a_re, a_im = al_ref[0, 0:1, :], al_ref[0, 1:2, :]

    def step(c, carry):
        out = []
        for b in range(bsz):
            h_re, h_im = carry[b]
            r = b * nc + c
            loc_re, loc_im = sre_ref[pl.ds(r, 1), :], sim_ref[pl.ds(r, 1), :]
            sre_ref[pl.ds(r, 1), :] = h_re
            sim_ref[pl.ds(r, 1), :] = h_im
            out.append((a_re * h_re - a_im * h_im + loc_re, a_re * h_im + a_im * h_re + loc_im))
        return tuple(out)

    zero = jnp.zeros((1, S5_STATE), F32)
    lax.fori_loop(0, nc, step, tuple((zero, zero) for _ in range(bsz)))
    y = _dot(u, tt_ref[0])
    y = y + _dot_nt(sre_ref[...].astype(BF16), car_ref[0])
    y = y - _dot_nt(sim_ref[...].astype(BF16), cai_ref[0])

    for k in range(chunk):
        col = k // S5_OCT
        rolled = pltpu.roll(words(y[:, col * LANES:(col + 1) * LANES]), shift(g, k % S5_OCT), axis=1)
        acc_ref[k] = jnp.where(lane_grp == g, rolled, acc_ref[k])

    @pl.when(g == S5_OCT - 1)
    def _():
        for k in range(chunk):
            yk = pltpu.bitcast(acc_ref[k], BF16).astype(F32)
            z_ref[k] = jax.nn.gelu(yk + d_ref[...] * x_ref[k]).astype(z_ref.dtype)


def _s5_scan(x, tables, d_skip, bsz):
    tt, wbr, wbi, car, cai, al = tables
    chunk, rows, width = x.shape
    pst = S5_STATE
    tw = chunk * S5_GROUP
    tab = lambda o, g: (o * S5_OCT + g, 0, 0)
    spec_w = pl.BlockSpec((1, tw, pst), tab)
    blk = pl.BlockSpec((chunk, rows, LANES), lambda o, g: (0, 0, o))
    return pl.pallas_call(
        functools.partial(_s5_scan_body, bsz=bsz, nc=rows // bsz, chunk=chunk),
        grid=(width // LANES, S5_OCT),
        in_specs=[blk, pl.BlockSpec((1, tw, tw), tab), spec_w, spec_w, spec_w, spec_w,
                  pl.BlockSpec((1, 2, pst), tab),
                  pl.BlockSpec((1, LANES), lambda o, g: (0, o))],
        out_specs=blk,
        out_shape=jax.ShapeDtypeStruct((chunk, rows, width), BF16),
        scratch_shapes=[pltpu.VMEM((chunk, rows // 2, LANES), jnp.uint32),
                        pltpu.VMEM((chunk, rows // 2, LANES), jnp.uint32), pltpu.VMEM((rows, tw), BF16),
                        pltpu.VMEM((rows, pst), F32), pltpu.VMEM((rows, pst), F32)],
        compiler_params=_params("parallel", "arbitrary"), name="s5_scan",
    )(x, tt, wbr, wbi, car, cai, al, d_skip.reshape(1, width).astype(F32))


def _to_position_major(t, chunk):
    m, width = t.shape
    return t.reshape(m // chunk, chunk, width).transpose(1, 0, 2).reshape(m, width)


def _from_position_major(t, chunk):
    m, width = t.shape
    return t.reshape(chunk, m // chunk, width).transpose(1, 0, 2).reshape(m, width)


def _s5_mixer(u_pm, lam_re, lam_im, log_dt, b_re, b_im, c_re, c_im, d_skip, w_glu, layer, b_glu,
              bsz, chunk):
    m, width = u_pm.shape
    tables = _s5_build(lam_re, lam_im, log_dt, b_re, b_im, c_re, c_im, chunk)
    z = _s5_scan(u_pm.reshape(chunk, m // chunk, width), tables, d_skip, bsz).reshape(m, width)
    return _matmul(z, w_glu, layer=layer, out_dtype=BF16, act="glu", z=z, bias=b_glu,
                   tn=min(width, 768), tk=width)


def _gdn_conv_body(x_ref, halo_ref, w_ref, o_ref, buf_ref, *, ts, nh):
    i = pl.program_id(1)
    j = pl.program_id(2)
    halo = halo_ref[0]
    buf_ref[0:8, :] = jnp.where(i == 0, jnp.zeros_like(halo), halo)
    buf_ref[8:8 + ts, :] = x_ref[0]
    acc = None
    for tap in range(GDN_CONV):
        off = 8 - (GDN_CONV - 1) + tap
        term = buf_ref[off:off + ts, :] * w_ref[tap:tap + 1, :]
        acc = term if acc is None else acc + term
    y = acc * _sigmoid(acc)
    qscale = jnp.where(j == 0, HEAD_DIM ** -0.5, 1.0).astype(F32)
    for h in range(nh):
        sl = slice(h * HEAD_DIM, (h + 1) * HEAD_DIM)
        yh = y[:, sl]
        nrm = yh * lax.rsqrt(jnp.sum(yh * yh, axis=-1, keepdims=True) + EPS) * qscale
        o_ref[0, :, sl] = jnp.where(j == 2, yh, nrm)


def _gdn_conv(x, conv_w, ts=256):
    bsz, seq, w4 = x.shape
    wd = w4 // 4
    w3 = 3 * wd
    ts = min(ts, seq)
    per8 = ts // 8
    return pl.pallas_call(
        functools.partial(_gdn_conv_body, ts=ts, nh=wd // HEAD_DIM),
        grid=(bsz, seq // ts, 3),
        in_specs=[pl.BlockSpec((1, ts, wd), lambda b, i, j: (b, i, j)),
                  pl.BlockSpec((1, 8, wd), lambda b, i, j: (b, jnp.maximum(i * per8 - 1, 0), j)),
                  pl.BlockSpec((GDN_CONV, wd), lambda b, i, j: (0, j))],
        out_specs=pl.BlockSpec((1, ts, wd), lambda b, i, j: (b, i, j)),
        out_shape=jax.ShapeDtypeStruct((bsz, seq, w3), F32),
        scratch_shapes=[pltpu.VMEM((ts + 8, wd), F32)],
        compiler_params=_params("parallel", "parallel", "parallel"), name="gdn_conv",
    )(x, x, conv_w.astype(F32))


_G_GC, _G_EGC, _G_EDEC, _G_BETA, _G_GLAST = 0, 16, 32, 48, 64


def _gdn_gates_body(x_ref, alog_ref, dtb_ref, o_ref, *, ts):
    x = x_ref[...]
    g = -jnp.exp(alog_ref[...]) * _softplus(x + dtb_ref[...])
    r = lax.broadcasted_iota(jnp.int32, (ts, ts), 0)
    c = lax.broadcasted_iota(jnp.int32, (ts, ts), 1)
    same = (r // GDN_CHUNK) == (c // GDN_CHUNK)
    gc = _dot(jnp.where(same & (r >= c), 1.0, 0.0).astype(F32), g, HI)
    gl = _dot(jnp.where(same, 1.0, 0.0).astype(F32), g, HI)
    lane = lax.broadcasted_iota(jnp.int32, x.shape, 1)
    out = jnp.where(lane < _G_EGC, gc,
          jnp.where(lane < _G_EDEC, jnp.exp(gc),
          jnp.where(lane < _G_BETA, jnp.exp(gl - gc),
          jnp.where(lane < _G_GLAST, _sigmoid(x), jnp.exp(gl)))))
    o_ref[...] = out


def _gdn_gates(ab, a_log, dt_bias, ts=512):
    m = ab.shape[0]
    ts = min(ts, m)
    nh = a_log.shape[0]

    def lanes(p):
        row = jnp.zeros((LANES,), F32)
        for off in (_G_GC, _G_EGC, _G_EDEC, _G_GLAST):
            row = row.at[off:off + nh].set(p.astype(F32))
        return row.reshape(1, LANES)

    spec = pl.BlockSpec((ts, LANES), lambda i: (i, 0))
    prm = pl.BlockSpec((1, LANES), lambda i: (0, 0))
    return pl.pallas_call(
        functools.partial(_gdn_gates_body, ts=ts), grid=(m // ts,),
        in_specs=[spec, prm, prm], out_specs=spec,
        out_shape=jax.ShapeDtypeStruct((m, LANES), F32),
        compiler_params=_params("parallel"), name="gdn_gates",
    )(ab, lanes(a_log), lanes(dt_bias))


def _split3(x):
    hi = x.astype(BF16)
    lo = (x - hi.astype(F32)).astype(BF16)
    return hi, lo


def _dot3(a, b):
    a_hi, a_lo = _split3(a)
    b_hi, b_lo = _split3(b)
    lhs = jnp.concatenate([a_hi, a_lo, a_hi], axis=1)
    rhs = jnp.concatenate([b_hi, b_hi, b_lo], axis=0)
    return _dot(lhs, rhs)


def _gdn_intra_body(q_ref, k_ref, v_ref, gp_ref, gct_ref, u_ref, w_ref, qd_ref, kd_ref, at_ref,
                    *, ngrp, grows):
    cl = GDN_CHUNK
    h = pl.program_id(1)
    gp = gp_ref[0]
    gc_col = _lane_pick(gp, h + _G_GC)
    egc_col = _lane_pick(gp, h + _G_EGC)
    edec_col = _lane_pick(gp, h + _G_EDEC)
    beta_col = _lane_pick(gp, h + _G_BETA)
    r = lax.broadcasted_iota(jnp.int32, (grows, grows), 0)
    c = lax.broadcasted_iota(jnp.int32, (grows, grows), 1)
    same = (r // cl) == (c // cl)
    lower = same & (r >= c)
    strict = same & (r > c)
    steps = int(math.log2(cl))
    ps, sols = [], []
    for g in range(ngrp):
        rows = slice(g * grows, (g + 1) * grows)
        q, k, v = q_ref[0, rows, :], k_ref[0, rows, :], v_ref[0, rows, :]
        beta, egc = beta_col[rows], egc_col[rows]
        gc_row = gct_ref[0, 0, :, rows]
        decay = jnp.exp(jnp.where(lower, gc_col[rows] - gc_row, -jnp.inf))
        kb, vb = k * beta, v * beta
        kbf = k.astype(BF16)
        ps.append(-jnp.where(strict, _dot_nt(kb.astype(BF16), kbf) * decay, 0.0))
        sols.append(jnp.concatenate([vb, kb * egc], axis=-1))
        qd_ref[0, rows, :] = (q * egc).astype(qd_ref.dtype)
        kd_ref[0, rows, :] = (k * edec_col[rows]).astype(kd_ref.dtype)
        attn = jnp.where(lower, _dot_nt(q.astype(BF16), kbf) * decay, 0.0)
        for n in range(grows // cl):
            at_ref[0, 0, g * grows + n * cl:g * grows + (n + 1) * cl, :] = (
                attn[n * cl:(n + 1) * cl, n * cl:(n + 1) * cl].astype(at_ref.dtype))
    for s in range(steps):
        for g in range(ngrp):
            sols[g] = sols[g] + _dot3(ps[g], sols[g])
            if s + 1 < steps:
                ps[g] = _dot3(ps[g], ps[g])
    for g in range(ngrp):
        rows = slice(g * grows, (g + 1) * grows)
        u_ref[0, rows, :] = sols[g][:, :HEAD_DIM]
        w_ref[0, rows, :] = sols[g][:, HEAD_DIM:].astype(w_ref.dtype)


def _gdn_intra(qkv, gates, gct, ngrp=4, grows=256):
    bsz, seq, w3 = qkv.shape
    wd = w3 // 3
    nh = wd // HEAD_DIM
    cl = GDN_CHUNK
    grows = min(grows, seq)
    ngrp = min(ngrp, seq // grows)
    tt = ngrp * grows
    col = lambda off: pl.BlockSpec((1, tt, HEAD_DIM), lambda b, h, i: (b, i, h + off))
    tok_sd = lambda dt: jax.ShapeDtypeStruct((bsz, seq, wd), dt)
    return pl.pallas_call(
        functools.partial(_gdn_intra_body, ngrp=ngrp, grows=grows), grid=(bsz, nh, seq // tt),
        in_specs=[col(0), col(nh), col(2 * nh),
                  pl.BlockSpec((1, tt, LANES), lambda b, h, i: (b, i, 0)),
                  pl.BlockSpec((1, 1, 1, tt), lambda b, h, i: (b, h, 0, i))],
        out_specs=[col(0), col(0), col(0), col(0),
                   pl.BlockSpec((1, 1, tt, cl), lambda b, h, i: (b, h, i, 0))],
        out_shape=[tok_sd(F32), tok_sd(BF16), tok_sd(BF16), tok_sd(BF16),
                   jax.ShapeDtypeStruct((bsz, nh, seq, cl), BF16)],
        compiler_params=_params("parallel", "parallel", "parallel"), name="gdn_intra",
    )(qkv, qkv, qkv, gates, gct)


def _gdn_scan_body(gl_ref, u_ref, w_ref, qd_ref, kd_ref, at_ref, gate_ref, on_ref, o_ref,
                   state_ref, *, nb, nh):
    cl = GDN_CHUNK
    b = pl.program_id(0)
    i = pl.program_id(1)

    @pl.when(i == 0)
    def _():
        state_ref[...] = jnp.zeros_like(state_ref)

    for n in range(nb):
        rows = slice(n * cl, (n + 1) * cl)
        from_state = []
        for h in range(nh):
            sl = slice(h * HEAD_DIM, (h + 1) * HEAD_DIM)
            wq = jnp.concatenate([w_ref[0, rows, sl], qd_ref[0, rows, sl]], axis=0).astype(BF16)
            from_state.append(_dot(wq, state_ref[h].astype(BF16)))
        for h in range(nh):
            sl = slice(h * HEAD_DIM, (h + 1) * HEAD_DIM)
            v_new = u_ref[0, rows, sl] - from_state[h][:cl]
            vb = v_new.astype(BF16)
            out = from_state[h][cl:] + _dot(at_ref[0, h, rows, :].astype(BF16), vb)
            g_last = gl_ref[b, i * nb + n, h]
            state_ref[h] = state_ref[h] * g_last + _dot_tn(kd_ref[0, rows, sl].astype(BF16), vb)
            gate = gate_ref[0, rows, sl]
            nrm = out * lax.rsqrt(jnp.mean(out * out, axis=-1, keepdims=True) + EPS) * on_ref[...]
            o_ref[0, rows, sl] = (nrm * (gate * _sigmoid(gate))).astype(o_ref.dtype)


def _gdn_scan(glast, u_c, w_c, q_dec, k_dec, attn, main, o_norm, nb=4):
    bsz, seq, wd = u_c.shape
    nh = wd // HEAD_DIM
    cl = GDN_CHUNK
    nb = min(nb, seq // cl)
    tt = nb * cl
    tok = pl.BlockSpec((1, tt, wd), lambda b, i: (b, i, 0))
    return pl.pallas_call(
        functools.partial(_gdn_scan_body, nb=nb, nh=nh), grid=(bsz, seq // tt),
        in_specs=[pl.BlockSpec(memory_space=pltpu.SMEM), tok, tok, tok, tok,
                  pl.BlockSpec((1, nh, tt, cl), lambda b, i: (b, 0, i, 0)),
                  pl.BlockSpec((1, tt, wd), lambda b, i: (b, i, 3)),
                  pl.BlockSpec((1, HEAD_DIM), lambda b, i: (0, 0))],
        out_specs=tok,
        out_shape=jax.ShapeDtypeStruct((bsz, seq, wd), BF16),
        scratch_shapes=[pltpu.VMEM((nh, HEAD_DIM, HEAD_DIM), F32)],
        compiler_params=_params("parallel", "arbitrary"), name="gdn_scan",
    )(glast, u_c, w_c, q_dec, k_dec, attn, main, o_norm.reshape(1, HEAD_DIM).astype(F32))


def _gdn_mixer(main, ab, conv_w, a_log, dt_bias, o_norm):
    bsz, seq, w4 = main.shape
    nh = w4 // 4 // HEAD_DIM
    qkv = _gdn_conv(main, conv_w)
    gates = _gdn_gates(ab, a_log, dt_bias).reshape(bsz, seq, LANES)
    gct = gates[:, :, _G_GC:_G_GC + nh].transpose(0, 2, 1).reshape(bsz, nh, 1, seq)
    glast = gates[:, GDN_CHUNK - 1::GDN_CHUNK, _G_GLAST:_G_GLAST + nh]
    u_c, w_c, q_dec, k_dec, attn = _gdn_intra(qkv, gates, gct)
    return _gdn_scan(glast, u_c, w_c, q_dec, k_dec, attn, main, o_norm)


def _fox_cumf_body(x_ref, bf_ref, o_ref, carry_ref, *, ts):
    @pl.when(pl.program_id(1) == 0)
    def _():
        carry_ref[...] = jnp.zeros_like(carry_ref)

    ls = -_softplus(-(x_ref[0] + bf_ref[...]))
    r = lax.broadcasted_iota(jnp.int32, (ts, ts), 0)
    c = lax.broadcasted_iota(jnp.int32, (ts, ts), 1)
    cum = _dot(jnp.where(r >= c, 1.0, 0.0).astype(F32), ls, HI) + carry_ref[...]
    o_ref[0] = cum
    carry_ref[...] = cum[ts - 1:ts, :]


def _fox_cumf(f_logit, b_f, ts=256):
    bsz, seq, _ = f_logit.shape
    ts = min(ts, seq)
    bf = jnp.zeros((LANES,), F32).at[:b_f.shape[0]].set(b_f.astype(F32)).reshape(1, LANES)
    spec = pl.BlockSpec((1, ts, LANES), lambda b, i: (b, i, 0))
    return pl.pallas_call(
        functools.partial(_fox_cumf_body, ts=ts), grid=(bsz, seq // ts),
        in_specs=[spec, pl.BlockSpec((1, LANES), lambda b, i: (0, 0))],
        out_specs=spec, out_shape=jax.ShapeDtypeStruct((bsz, seq, LANES), F32),
        scratch_shapes=[pltpu.VMEM((1, LANES), F32)],
        compiler_params=_params("parallel", "arbitrary"), name="fox_cumf",
    )(f_logit, bf)


LOG2E = math.log2(math.e)
FOX_QSCALE = HEAD_DIM ** -0.5 * LOG2E


def _fox_attn_body(q_ref, k_ref, v_ref, cft_ref, o_ref, *scratch, tq, nblk):
    qi = pl.program_id(2)
    tb = tq // nblk
    m_refs, l_refs, acc_refs = scratch[0::3], scratch[1::3], scratch[2::3]
    for blk in range(nblk):
        m_refs[blk][...] = jnp.full_like(m_refs[blk], -jnp.inf)
        l_refs[blk][...] = jnp.zeros_like(l_refs[blk])
        acc_refs[blk][...] = jnp.zeros_like(acc_refs[blk])

    def scores(blk, kb):
        return _dot_nt(q_ref[0, blk * tb:(blk + 1) * tb, :], kb)

    def softmax_step(blk, s, ck, masked):
        m_ref, l_ref = m_refs[blk], l_refs[blk]
        s = s - ck
        if masked:
            r = lax.broadcasted_iota(jnp.int32, s.shape, 0) + blk * tb
            c = lax.broadcasted_iota(jnp.int32, s.shape, 1)
            s = jnp.where(r >= c, s, -jnp.inf)
        m_prev = m_ref[...]
        m_new = jnp.maximum(m_prev, jnp.max(s, axis=-1, keepdims=True))
        alpha = jnp.exp2(m_prev - m_new)
        p = jnp.exp2(s - m_new)
        l_ref[...] = alpha * l_ref[...] + jnp.sum(p, axis=-1, keepdims=True)
        m_ref[...] = m_new
        return p.astype(BF16), alpha

    def accumulate(blk, p, alpha, vb):
        acc_refs[blk][...] = alpha * acc_refs[blk][...] + _dot(p, vb)

    def body(j, carry):
        start = pl.multiple_of(j * tq, tq)
        kb, vb = k_ref[0, pl.ds(start, tq), :], v_ref[0, pl.ds(start, tq), :]
        ck = cft_ref[0, 0, j] * LOG2E
        s_all = [scores(blk, kb) for blk in range(nblk)]
        for blk in range(nblk):
            p, alpha = softmax_step(blk, s_all[blk], ck, False)
            accumulate(blk, p, alpha, vb)
        return carry

    lax.fori_loop(0, qi, body, 0)
    d0 = pl.multiple_of(qi * tq, tq)
    ck = cft_ref[0, 0, qi] * LOG2E
    nkeys = [(blk + 1) * tb for blk in range(nblk)]
    for blk in range(nblk):
        s_cur = scores(blk, k_ref[0, pl.ds(d0, nkeys[blk]), :])
        p, alpha = softmax_step(blk, s_cur, ck[:, :nkeys[blk]], True)
        accumulate(blk, p, alpha, v_ref[0, pl.ds(d0, nkeys[blk]), :])
        o_ref[0, blk * tb:(blk + 1) * tb, :] = (acc_refs[blk][...] / l_refs[blk][...]).astype(o_ref.dtype)


def _fox_attention(qkv, cumf_t, tq, nblk=4):
    bsz, seq, w3 = qkv.shape
    wd = w3 // 3
    nh = wd // HEAD_DIM
    nq = seq // tq
    tb = tq // nblk
    kv = lambda off: pl.BlockSpec((1, seq, HEAD_DIM), lambda b, h, qi: (b, 0, h + off))
    scratch = []
    for _ in range(nblk):
        scratch += [pltpu.VMEM((tb, 1), F32), pltpu.VMEM((tb, 1), F32), pltpu.VMEM((tb, HEAD_DIM), F32)]
    return pl.pallas_call(
        functools.partial(_fox_attn_body, tq=tq, nblk=nblk), grid=(bsz, nh, nq),
        in_specs=[pl.BlockSpec((1, tq, HEAD_DIM), lambda b, h, qi: (b, qi, h)),
                  kv(nh), kv(2 * nh),
                  pl.BlockSpec((1, 1, nq, 1, tq), lambda b, h, qi: (b, h, 0, 0, 0))],
        out_specs=pl.BlockSpec((1, tq, HEAD_DIM), lambda b, h, qi: (b, qi, h)),
        out_shape=jax.ShapeDtypeStruct((bsz, seq, wd), BF16),
        scratch_shapes=scratch,
        compiler_params=_params("parallel", "parallel", "arbitrary"),
        name="fox_attention",
    )(qkv, qkv, qkv, cumf_t)


def _fox_mixer(qkv, f_logit, b_f, tq=1024):
    bsz, seq, w3 = qkv.shape
    nh = w3 // 3 // HEAD_DIM
    tq = min(tq, seq)
    cumf = _fox_cumf(f_logit, b_f)
    cumf_t = cumf[:, :, :nh].transpose(0, 2, 1).reshape(bsz, nh, seq // tq, 1, tq)
    return _fox_attention(qkv, cumf_t, tq)


def _pad_cols(w, groups):
    out = jnp.zeros((w.shape[0], LANES), w.dtype)
    for off in groups:
        out = out.at[:, off:off + w.shape[1]].set(w)
    return out


def kernel(x, mem, mem_norm, w_mem_kv, norm1, w_out, norm2, w_up, w_down, norm_f,
           s5_w_in, s5_lam_re, s5_lam_im, s5_log_dt, s5_b_re, s5_b_im, s5_c_re, s5_c_im,
           s5_d_skip, s5_w_glu, s5_b_glu,
           gdn_w_in, gdn_conv_w, gdn_a_log, gdn_dt_bias, gdn_o_norm,
           fox_w_in, fox_b_f):
    bsz, seq, d = x.shape
    m = bsz * seq
    depth = norm1.shape[0]
    wd = d - MEM_WIDTH
    nh = wd // HEAD_DIM
    mlen = mem.shape[1]

    mem_a = _rmsnorm(mem.reshape(bsz * mlen, d), mem_norm)
    mkv = _matmul(mem_a, w_mem_kv.astype(BF16)[None], out_dtype=BF16, tn=512)
    mem_k = mkv[:, :MEM_WIDTH].reshape(bsz, mlen, MEM_WIDTH)
    mem_v = mkv[:, MEM_WIDTH:].reshape(bsz, mlen, MEM_WIDTH)

    w_out_b, w_up_b, w_down_b = w_out.astype(BF16), w_up.astype(BF16), w_down.astype(BF16)
    s5_w_in_b, s5_w_glu_b = s5_w_in.astype(BF16), s5_w_glu.astype(BF16)
    gdn_w_in_b, fox_w_in_b = gdn_w_in.astype(BF16), fox_w_in.astype(BF16)

    h = x.reshape(m, d)
    a = _rmsnorm(h, norm1[0])
    out = None
    for i in range(depth):
        kind, j = i % 3, i // 3
        if kind == 0:
            chunk = min(S5_CHUNK, seq)
            u = _matmul(_to_position_major(a, chunk), s5_w_in_b, layer=j, n=wd, out_dtype=F32, tn=768)
            mq = _matmul(a, s5_w_in_b, layer=j, col0=wd // MEM_WIDTH, n=MEM_WIDTH, out_dtype=BF16,
                         tn=MEM_WIDTH)
            mix = _s5_mixer(u, s5_lam_re[j], s5_lam_im[j], s5_log_dt[j], s5_b_re[j], s5_b_im[j],
                            s5_c_re[j], s5_c_im[j], s5_d_skip[j], s5_w_glu_b, j, s5_b_glu[j],
                            bsz, chunk)
            mix = _from_position_major(mix, chunk)
        elif kind == 1:
            w_in = gdn_w_in[j]
            w_ab = (_pad_cols(w_in[:, 4 * wd:4 * wd + nh], (_G_GC, _G_EGC, _G_EDEC, _G_GLAST))
                    + _pad_cols(w_in[:, 4 * wd + nh:4 * wd + 2 * nh], (_G_BETA,)))
            main = _matmul(a, gdn_w_in_b, layer=j, n=4 * wd, out_dtype=F32, tn=768)
            ab = _matmul(a, w_ab.astype(BF16)[None], out_dtype=F32)
            mq = _matmul(a, w_in[:, -MEM_WIDTH:].astype(BF16)[None], out_dtype=BF16, tn=MEM_WIDTH)
            main = main.reshape(bsz, seq, 4 * wd)
            mix = _gdn_mixer(main, ab, gdn_conv_w[j], gdn_a_log[j], gdn_dt_bias[j],
                             gdn_o_norm[j]).reshape(m, wd)
        else:
            w_in = fox_w_in[j]
            qscale = jnp.concatenate([jnp.full((wd,), FOX_QSCALE, F32), jnp.ones((2 * wd,), F32)])
            qkv = _matmul(a, fox_w_in_b, layer=j, n=3 * wd, out_dtype=BF16, act="scale",
                          bias=qscale, tn=768)
            w_f = _pad_cols(w_in[:, 3 * wd:3 * wd + nh], (0,))
            fl = _matmul(a, w_f.astype(BF16)[None], out_dtype=F32)
            mq = _matmul(a, w_in[:, -MEM_WIDTH:].astype(BF16)[None], out_dtype=BF16, tn=MEM_WIDTH)
            mix = _fox_mixer(qkv.reshape(bsz, seq, 3 * wd), fl.reshape(bsz, seq, LANES),
                             fox_b_f[j]).reshape(m, wd)
        read = _mem_attention(mq.reshape(bsz, seq, MEM_WIDTH), mem_k, mem_v).reshape(m, MEM_WIDTH)
        h, a = _matmul_cat_norm(mix, read, w_out_b, i, h, norm2[i])
        up = _matmul(a, w_up_b, layer=i, out_dtype=BF16, act="relu2")
        if i + 1 < depth:
            h, a = _matmul_res_norm(up, w_down_b, i, h, norm1[i + 1], norm_dtype=BF16, keep_h=True)
        else:
            _, out = _matmul_res_norm(up, w_down_b, i, h, norm_f, norm_dtype=x.dtype, keep_h=False)
    return out.reshape(bsz, seq, d)
```

```python
import functools
import math

import jax
import jax.numpy as jnp
from jax import lax
from jax.experimental import pallas as pl
from jax.experimental.pallas import tpu as pltpu

F32 = jnp.float32
BF16 = jnp.bfloat16
EPS = 1e-6
HEAD_DIM = 128
LANES = 128
MEM_HEADS = 4
MEM_WIDTH = MEM_HEADS * HEAD_DIM
S5_GROUP = 16
S5_STATE = 64
S5_CHUNK = 64
GDN_CHUNK = 64
GDN_CONV = 4
VMEM_LIMIT_BYTES = 48 * 1024 * 1024
HI = lax.Precision.HIGHEST


def _params(*sem):
    return pltpu.CompilerParams(dimension_semantics=sem, vmem_limit_bytes=VMEM_LIMIT_BYTES)


def _dot(a, b, precision=None):
    return jnp.dot(a, b, preferred_element_type=F32, precision=precision)


def _dot_nt(a, b, precision=None):
    return lax.dot_general(a, b, (((1,), (1,)), ((), ())), preferred_element_type=F32,
                           precision=precision)


def _dot_tn(a, b, precision=None):
    return lax.dot_general(a, b, (((0,), (0,)), ((), ())), preferred_element_type=F32,
                           precision=precision)


def _softplus(x):
    return jnp.maximum(x, 0.0) + jnp.log1p(jnp.exp(-jnp.abs(x)))


def _sigmoid(x):
    return 1.0 / (1.0 + jnp.exp(-x))


def _lane_pick(x, lane):
    ids = lax.broadcasted_iota(jnp.int32, x.shape, 1)
    return jnp.sum(jnp.where(ids == lane, x, 0.0), axis=-1, keepdims=True)


def _rmsnorm_body(x_ref, g_ref, o_ref):
    x = x_ref[...].astype(F32)
    ms = jnp.mean(x * x, axis=-1, keepdims=True)
    o_ref[...] = (x * lax.rsqrt(ms + EPS) * g_ref[...]).astype(o_ref.dtype)


def _rmsnorm(x2d, gain, out_dtype=BF16, tm=512):
    m, d = x2d.shape
    tm = min(tm, m)
    return pl.pallas_call(
        _rmsnorm_body, grid=(m // tm,),
        in_specs=[pl.BlockSpec((tm, d), lambda i: (i, 0)),
                  pl.BlockSpec((1, d), lambda i: (0, 0))],
        out_specs=pl.BlockSpec((tm, d), lambda i: (i, 0)),
        out_shape=jax.ShapeDtypeStruct((m, d), out_dtype),
        compiler_params=_params("parallel"), name="rmsnorm",
    )(x2d, gain.reshape(1, d).astype(F32))


def _mm_body(a_ref, w_ref, *refs, glu):
    y = _dot(a_ref[...], w_ref[...])
    if glu:
        z_ref, b_ref = refs[0], refs[1]
        y = z_ref[...].astype(F32) * _sigmoid(y + b_ref[...])
    o_ref = refs[-1]
    o_ref[...] = y.astype(o_ref.dtype)


def _matmul(a, w, *, out_dtype, layer=0, col0=0, n=None, glu_z=None, glu_bias=None, tm=1024, tn=1024):
    m, kdim = a.shape
    n = w.shape[2] if n is None else n
    tm, tn = min(tm, m), min(tn, n)
    assert m % tm == 0 and n % tn == 0 and w.shape[1] == kdim, (a.shape, w.shape, tm, tn)
    tile = pl.BlockSpec((tm, tn), lambda i, j: (i, j))
    in_specs = [pl.BlockSpec((tm, kdim), lambda i, j: (i, 0)),
                pl.BlockSpec((None, kdim, tn), lambda i, j: (layer, 0, j + col0))]
    args = [a, w]
    if glu_z is not None:
        in_specs += [tile, pl.BlockSpec((1, tn), lambda i, j: (0, j))]
        args += [glu_z, glu_bias.reshape(1, n).astype(F32)]
    return pl.pallas_call(
        functools.partial(_mm_body, glu=glu_z is not None), grid=(m // tm, n // tn),
        in_specs=in_specs, out_specs=tile,
        out_shape=jax.ShapeDtypeStruct((m, n), out_dtype),
        compiler_params=_params("parallel", "parallel"), name="matmul",
    )(*args)


def _mm_wcast_body(a_ref, w_ref, *refs, act):
    b_ref = refs[0] if act == "scale" else None
    o_ref, wb_ref = refs[-2], refs[-1]

    @pl.when(pl.program_id(1) == 0)
    def _():
        wb_ref[...] = w_ref[...].astype(wb_ref.dtype)

    y = _dot(a_ref[...], wb_ref[...])
    if act == "relu2":
        y = jnp.square(jnp.maximum(y, 0.0))
    elif act == "scale":
        y = y * b_ref[...]
    o_ref[...] = y.astype(o_ref.dtype)


def _matmul_wcast(a, w, *, out_dtype, layer=0, n=None, act=None, bias=None, tm=1024, tn=1024):
    m, kdim = a.shape
    n = w.shape[2] if n is None else n
    tm, tn = min(tm, m), min(tn, n)
    assert m % tm == 0 and n % tn == 0 and w.shape[1] == kdim
    in_specs = [pl.BlockSpec((tm, kdim), lambda j, i: (i, 0)),
                pl.BlockSpec((None, kdim, tn), lambda j, i: (layer, 0, j))]
    args = [a, w]
    if act == "scale":
        in_specs.append(pl.BlockSpec((1, tn), lambda j, i: (0, j)))
        args.append(bias.reshape(1, n).astype(F32))
    return pl.pallas_call(
        functools.partial(_mm_wcast_body, act=act), grid=(n // tn, m // tm),
        in_specs=in_specs,
        out_specs=pl.BlockSpec((tm, tn), lambda j, i: (i, j)),
        out_shape=jax.ShapeDtypeStruct((m, n), out_dtype),
        scratch_shapes=[pltpu.VMEM((kdim, tn), BF16)],
        compiler_params=_params("parallel", "arbitrary"), name="matmul_wcast",
    )(*args)


def _norm_rows(y, gain):
    return y * lax.rsqrt(jnp.mean(y * y, axis=-1, keepdims=True) + EPS) * gain


def _mm_res_norm_body(a_ref, w_ref, res_ref, g_ref, *refs, nk, keep_h):
    acc_ref, norm_ref = refs if keep_h else refs[::-1]
    k = pl.program_id(1)

    @pl.when(k == 0)
    def _():
        acc_ref[...] = res_ref[...] + _dot(a_ref[...], w_ref[...])

    @pl.when(k > 0)
    def _():
        acc_ref[...] += _dot(a_ref[...], w_ref[...])

    @pl.when(k == nk - 1)
    def _():
        norm_ref[...] = _norm_rows(acc_ref[...], g_ref[...]).astype(norm_ref.dtype)


def _matmul_res_norm(a, w, layer, res, gain, *, norm_dtype, keep_h, tm=512, tk=2048):
    m, kdim = a.shape
    n = w.shape[2]
    tm, tk = min(tm, m), min(tk, kdim)
    assert m % tm == 0 and kdim % tk == 0
    nk = kdim // tk
    row = pl.BlockSpec((tm, n), lambda i, k: (i, 0))
    out_specs = [row] * (2 if keep_h else 1)
    out_shape = ([jax.ShapeDtypeStruct((m, n), F32)] if keep_h else []) + [
        jax.ShapeDtypeStruct((m, n), norm_dtype)]
    outs = pl.pallas_call(
        functools.partial(_mm_res_norm_body, nk=nk, keep_h=keep_h), grid=(m // tm, nk),
        in_specs=[pl.BlockSpec((tm, tk), lambda i, k: (i, k)),
                  pl.BlockSpec((None, tk, n), lambda i, k: (layer, k, 0)),
                  row, pl.BlockSpec((1, n), lambda i, k: (0, 0))],
        out_specs=out_specs, out_shape=out_shape,
        scratch_shapes=[] if keep_h else [pltpu.VMEM((tm, n), F32)],
        compiler_params=_params("parallel", "arbitrary"), name="matmul_res_norm",
    )(a, w, res, gain.reshape(1, n).astype(F32))
    return outs if keep_h else (None, outs[0])


def _mm_cat_norm_body(a1_ref, a2_ref, w1_ref, w2_ref, res_ref, g_ref, h_ref, n_ref):
    y = _dot(a1_ref[...], w1_ref[...]) + _dot(a2_ref[...], w2_ref[...]) + res_ref[...]
    h_ref[...] = y
    n_ref[...] = _norm_rows(y, g_ref[...]).astype(n_ref.dtype)


def _matmul_cat_norm(a1, a2, w, layer, res, gain, tm=512):
    m, k1 = a1.shape
    k2 = a2.shape[1]
    n = w.shape[2]
    tm = min(tm, m)
    assert m % tm == 0 and w.shape[1] == k1 + k2 and k1 % k2 == 0
    row = pl.BlockSpec((tm, n), lambda i: (i, 0))
    return pl.pallas_call(
        _mm_cat_norm_body, grid=(m // tm,),
        in_specs=[pl.BlockSpec((tm, k1), lambda i: (i, 0)),
                  pl.BlockSpec((tm, k2), lambda i: (i, 0)),
                  pl.BlockSpec((None, k1, n), lambda i: (layer, 0, 0)),
                  pl.BlockSpec((None, k2, n), lambda i: (layer, k1 // k2, 0)),
                  row, pl.BlockSpec((1, n), lambda i: (0, 0))],
        out_specs=[row, row],
        out_shape=[jax.ShapeDtypeStruct((m, n), F32), jax.ShapeDtypeStruct((m, n), BF16)],
        compiler_params=_params("parallel"), name="matmul_cat_norm",
    )(a1, a2, w, w, res, gain.reshape(1, n).astype(F32))


def _mem_attn_body(q_ref, k_ref, v_ref, o_ref):
    scale = HEAD_DIM ** -0.5
    for h in range(MEM_HEADS):
        sl = slice(h * HEAD_DIM, (h + 1) * HEAD_DIM)
        logits = _dot_nt(q_ref[0, :, sl], k_ref[0, :, sl]) * scale
        mx = jnp.max(logits, axis=-1, keepdims=True)
        e = jnp.exp(logits - mx)
        p = e / jnp.sum(e, axis=-1, keepdims=True)
        o_ref[0, :, sl] = _dot(p.astype(BF16), v_ref[0, :, sl]).astype(o_ref.dtype)


def _mem_attention(q, mem_k, mem_v, ts=1024):
    bsz, seq, _ = q.shape
    mlen = mem_k.shape[1]
    ts = min(ts, seq)
    return pl.pallas_call(
        _mem_attn_body, grid=(bsz, seq // ts),
        in_specs=[pl.BlockSpec((1, ts, MEM_WIDTH), lambda b, i: (b, i, 0)),
                  pl.BlockSpec((1, mlen, MEM_WIDTH), lambda b, i: (b, 0, 0)),
                  pl.BlockSpec((1, mlen, MEM_WIDTH), lambda b, i: (b, 0, 0))],
        out_specs=pl.BlockSpec((1, ts, MEM_WIDTH), lambda b, i: (b, i, 0)),
        out_shape=jax.ShapeDtypeStruct((bsz, seq, MEM_WIDTH), BF16),
        compiler_params=_params("parallel", "parallel"), name="mem_attention",
    )(q, mem_k, mem_v)


def _s5_build_body(lre_ref, lim_ref, ldt_ref, cre_ref, cim_ref, btr_ref, bti_ref,
                   tt_ref, wbr_ref, wbi_ref, car_ref, cai_ref, al_ref, r_ref, *, chunk):
    grp, pst = S5_GROUP, S5_STATE
    l_re, l_im = lre_ref[0], lim_ref[0]
    dt = jnp.exp(ldt_ref[0])
    x_re, x_im = l_re * dt, l_im * dt
    a_mag = jnp.exp(x_re)
    a_re, a_im = a_mag * jnp.cos(x_im), a_mag * jnp.sin(x_im)
    den = l_re * l_re + l_im * l_im
    z_re = ((a_re - 1.0) * l_re + a_im * l_im) / den
    z_im = (a_im * l_re - (a_re - 1.0) * l_im) / den
    bt_re, bt_im = btr_ref[0], bti_ref[0]
    bb_re = z_re * bt_re - z_im * bt_im
    bb_im = z_re * bt_im + z_im * bt_re
    c_re, c_im = cre_ref[0], cim_ref[0]

    sub = 8
    nhi = chunk // sub

    def closed_form(t):
        mag = jnp.exp(x_re[None] * t)
        ang = x_im[None] * t
        return mag * jnp.cos(ang), mag * jnp.sin(ang)

    def powers(reverse):
        a = lax.broadcasted_iota(jnp.int32, (nhi, 1, pst), 0)
        b = lax.broadcasted_iota(jnp.int32, (sub, 1, pst), 0)
        if reverse:
            a, b = nhi - 1 - a, sub - 1 - b
        hi_re, hi_im = closed_form((a * sub).astype(F32))
        lo_re, lo_im = closed_form(b.astype(F32))
        p_re = hi_re[:, None] * lo_re[None] - hi_im[:, None] * lo_im[None]
        p_im = hi_re[:, None] * lo_im[None] + hi_im[:, None] * lo_re[None]
        return p_re.reshape(chunk, 1, pst), p_im.reshape(chunk, 1, pst)

    def times(m_re, m_im, p_re, p_im):
        o_re = m_re[None] * p_re - m_im[None] * p_im
        o_im = m_re[None] * p_im + m_im[None] * p_re
        return o_re.reshape(chunk * grp, pst), o_im.reshape(chunk * grp, pst)

    p0_re, p0_im = powers(False)
    p1_re = p0_re * a_re[None] - p0_im * a_im[None]
    p1_im = p0_re * a_im[None] + p0_im * a_re[None]
    ca_re, ca_im = times(c_re, c_im, p1_re, p1_im)
    car_ref[0] = ca_re.astype(car_ref.dtype)
    cai_ref[0] = ca_im.astype(cai_ref.dtype)
    pr_re, pr_im = powers(True)
    wb_re, wb_im = times(bb_re, bb_im, pr_re, pr_im)
    wbr_ref[0] = wb_re.astype(wbr_ref.dtype)
    wbi_ref[0] = wb_im.astype(wbi_ref.dtype)
    lf = float(chunk)
    magl = jnp.exp(x_re * lf)
    al_ref[0] = jnp.concatenate([magl * jnp.cos(x_im * lf), magl * jnp.sin(x_im * lf)], axis=0)
    ce_re, ce_im = times(c_re, c_im, p0_re, p0_im)
    r_ref[...] = _dot_nt(bb_re, ce_re, HI) - _dot_nt(bb_im, ce_im, HI)
    width = chunk * grp
    lane = lax.broadcasted_iota(jnp.int32, (grp, width), 1)
    for j in range(chunk):
        r = r_ref[...]
        if j:
            r = jnp.where(lane >= j * grp, pltpu.roll(r, j * grp, axis=1), 0.0)
        tt_ref[0, j * grp:(j + 1) * grp, :] = r.astype(tt_ref.dtype)


def _s5_build(lam_re, lam_im, log_dt, b_re, b_im, c_re, c_im, chunk):
    ng, pst = lam_re.shape
    grp = S5_GROUP
    width = chunk * grp
    row = lambda x: x.astype(F32).reshape(ng, 1, pst)
    ldt = jnp.broadcast_to(log_dt.astype(F32)[:, None, None], (ng, 1, pst))
    args = (row(lam_re), row(lam_im), ldt, c_re.astype(F32), c_im.astype(F32),
            jnp.swapaxes(b_re.astype(F32), 1, 2), jnp.swapaxes(b_im.astype(F32), 1, 2))
    spec_row = pl.BlockSpec((1, 1, pst), lambda g: (g, 0, 0))
    spec_gp = pl.BlockSpec((1, grp, pst), lambda g: (g, 0, 0))
    spec_w = pl.BlockSpec((1, width, pst), lambda g: (g, 0, 0))
    return pl.pallas_call(
        functools.partial(_s5_build_body, chunk=chunk), grid=(ng,),
        in_specs=[spec_row, spec_row, spec_row, spec_gp, spec_gp, spec_gp, spec_gp],
        out_specs=[pl.BlockSpec((1, width, width), lambda g: (g, 0, 0)),
                   spec_w, spec_w, spec_w, spec_w,
                   pl.BlockSpec((1, 2, pst), lambda g: (g, 0, 0))],
        out_shape=[jax.ShapeDtypeStruct((ng, width, width), BF16),
                   jax.ShapeDtypeStruct((ng, width, pst), BF16),
                   jax.ShapeDtypeStruct((ng, width, pst), BF16),
                   jax.ShapeDtypeStruct((ng, width, pst), BF16),
                   jax.ShapeDtypeStruct((ng, width, pst), BF16),
                   jax.ShapeDtypeStruct((ng, 2, pst), F32)],
        scratch_shapes=[pltpu.VMEM((grp, width), F32)],
        compiler_params=_params("parallel"), name="s5_build",
    )(*args)


S5_OCT = LANES // S5_GROUP


def _s5_scan_body(x_ref, tt_ref, wbr_ref, wbi_ref, car_ref, cai_ref, al_ref, d_ref, z_ref,
                  acc_ref, xw_ref, ug_ref, sre_ref, sim_ref, *, bsz, nc, chunk):
    grp = S5_GROUP
    g = pl.program_id(1)
    rows = bsz * nc
    lane_grp = lax.broadcasted_iota(jnp.int32, (rows // 2, LANES), 1) // grp

    def shift(to_grp, from_grp):
        return lax.rem((to_grp - from_grp) * grp + LANES, LANES)

    def words(t):
        return pltpu.bitcast(t.astype(BF16), jnp.uint32)

    @pl.when(g == 0)
    def _():
        acc_ref[...] = jnp.zeros_like(acc_ref)
        for j in range(chunk):
            xw_ref[j] = words(x_ref[j])

    for col in range(chunk // S5_OCT):
        packed = jnp.zeros((rows // 2, LANES), jnp.uint32)
        for jj in range(S5_OCT):
            rolled = pltpu.roll(xw_ref[col * S5_OCT + jj], shift(jj, g), axis=1)
            packed = jnp.where(lane_grp == jj, rolled, packed)
        ug_ref[:, col * LANES:(col + 1) * LANES] = pltpu.bitcast(packed, BF16)

    u = ug_ref[...]
    sre_ref[...] = _dot(u, wbr_ref[0])
    sim_ref[...] = _dot(u, wbi_ref[0])
    a_re, a_im = al_ref[0, 0:1, :], al_ref[0, 1:2, :]

    def step(c, carry):
        out = []
        for b in range(bsz):
            h_re, h_im = carry[b]
            r = b * nc + c
            loc_re, loc_im = sre_ref[pl.ds(r, 1), :], sim_ref[pl.ds(r, 1), :]
            sre_ref[pl.ds(r, 1), :] = h_re
            sim_ref[pl.ds(r, 1), :] = h_im
            out.append((a_re * h_re - a_im * h_im + loc_re, a_re * h_im + a_im * h_re + loc_im))
        return tuple(out)

    zero = jnp.zeros((1, S5_STATE), F32)
    lax.fori_loop(0, nc, step, tuple((zero, zero) for _ in range(bsz)))
    y = _dot(u, tt_ref[0])
    y = y + _dot_nt(sre_ref[...].astype(BF16), car_ref[0])
    y = y - _dot_nt(sim_ref[...].astype(BF16), cai_ref[0])

    for k in range(chunk):
        col = k // S5_OCT
        rolled = pltpu.roll(words(y[:, col * LANES:(col + 1) * LANES]), shift(g, k % S5_OCT), axis=1)
        acc_ref[k] = jnp.where(lane_grp == g, rolled, acc_ref[k])

    @pl.when(g == S5_OCT - 1)
    def _():
        for k in range(chunk):
            yk = pltpu.bitcast(acc_ref[k], BF16).astype(F32)
            z_ref[k] = jax.nn.gelu(yk + d_ref[...] * x_ref[k]).astype(z_ref.dtype)


def _s5_scan(x, tables, d_skip, bsz):
    tt, wbr, wbi, car, cai, al = tables
    chunk, rows, width = x.shape
    pst = S5_STATE
    tw = chunk * S5_GROUP
    tab = lambda o, g: (o * S5_OCT + g, 0, 0)
    spec_w = pl.BlockSpec((1, tw, pst), tab)
    blk = pl.BlockSpec((chunk, rows, LANES), lambda o, g: (0, 0, o))
    return pl.pallas_call(
        functools.partial(_s5_scan_body, bsz=bsz, nc=rows // bsz, chunk=chunk),
        grid=(width // LANES, S5_OCT),
        in_specs=[blk, pl.BlockSpec((1, tw, tw), tab), spec_w, spec_w, spec_w, spec_w,
                  pl.BlockSpec((1, 2, pst), tab),
                  pl.BlockSpec((1, LANES), lambda o, g: (0, o))],
        out_specs=blk,
        out_shape=jax.ShapeDtypeStruct((chunk, rows, width), BF16),
        scratch_shapes=[pltpu.VMEM((chunk, rows // 2, LANES), jnp.uint32),
                        pltpu.VMEM((chunk, rows // 2, LANES), jnp.uint32), pltpu.VMEM((rows, tw), BF16),
                        pltpu.VMEM((rows, pst), F32), pltpu.VMEM((rows, pst), F32)],
        compiler_params=_params("parallel", "arbitrary"), name="s5_scan",
    )(x, tt, wbr, wbi, car, cai, al, d_skip.reshape(1, width).astype(F32))


def _to_position_major(t, chunk):
    m, width = t.shape
    return t.reshape(m // chunk, chunk, width).transpose(1, 0, 2).reshape(m, width)


def _from_position_major(t, chunk):
    m, width = t.shape
    return t.reshape(chunk, m // chunk, width).transpose(1, 0, 2).reshape(m, width)


def _s5_mixer(u_pm, lam_re, lam_im, log_dt, b_re, b_im, c_re, c_im, d_skip, w_glu, layer, b_glu,
              bsz, chunk):
    m, width = u_pm.shape
    tables = _s5_build(lam_re, lam_im, log_dt, b_re, b_im, c_re, c_im, chunk)
    z = _s5_scan(u_pm.reshape(chunk, m // chunk, width), tables, d_skip, bsz).reshape(m, width)
    return _matmul(z, w_glu, layer=layer, out_dtype=BF16, glu_z=z, glu_bias=b_glu, tn=min(width, 768))


def _gdn_conv_body(x_ref, halo_ref, w_ref, o_ref, buf_ref, *, ts, nh):
    i = pl.program_id(1)
    j = pl.program_id(2)
    halo = halo_ref[0]
    buf_ref[0:8, :] = jnp.where(i == 0, jnp.zeros_like(halo), halo)
    buf_ref[8:8 + ts, :] = x_ref[0]
    acc = None
    for tap in range(GDN_CONV):
        off = 8 - (GDN_CONV - 1) + tap
        term = buf_ref[off:off + ts, :] * w_ref[tap:tap + 1, :]
        acc = term if acc is None else acc + term
    y = acc * _sigmoid(acc)
    qscale = jnp.where(j == 0, HEAD_DIM ** -0.5, 1.0).astype(F32)
    for h in range(nh):
        sl = slice(h * HEAD_DIM, (h + 1) * HEAD_DIM)
        yh = y[:, sl]
        nrm = yh * lax.rsqrt(jnp.sum(yh * yh, axis=-1, keepdims=True) + EPS) * qscale
        o_ref[0, :, sl] = jnp.where(j == 2, yh, nrm)


def _gdn_conv(x, conv_w, ts=256):
    bsz, seq, w4 = x.shape
    wd = w4 // 4
    w3 = 3 * wd
    ts = min(ts, seq)
    per8 = ts // 8
    return pl.pallas_call(
        functools.partial(_gdn_conv_body, ts=ts, nh=wd // HEAD_DIM),
        grid=(bsz, seq // ts, 3),
        in_specs=[pl.BlockSpec((1, ts, wd), lambda b, i, j: (b, i, j)),
                  pl.BlockSpec((1, 8, wd), lambda b, i, j: (b, jnp.maximum(i * per8 - 1, 0), j)),
                  pl.BlockSpec((GDN_CONV, wd), lambda b, i, j: (0, j))],
        out_specs=pl.BlockSpec((1, ts, wd), lambda b, i, j: (b, i, j)),
        out_shape=jax.ShapeDtypeStruct((bsz, seq, w3), F32),
        scratch_shapes=[pltpu.VMEM((ts + 8, wd), F32)],
        compiler_params=_params("parallel", "parallel", "parallel"), name="gdn_conv",
    )(x, x, conv_w.astype(F32))


_G_GC, _G_EGC, _G_EDEC, _G_BETA, _G_GLAST = 0, 16, 32, 48, 64


def _gdn_gates_body(x_ref, alog_ref, dtb_ref, o_ref, *, ts):
    x = x_ref[...]
    g = -jnp.exp(alog_ref[...]) * _softplus(x + dtb_ref[...])
    r = lax.broadcasted_iota(jnp.int32, (ts, ts), 0)
    c = lax.broadcasted_iota(jnp.int32, (ts, ts), 1)
    same = (r // GDN_CHUNK) == (c // GDN_CHUNK)
    gc = _dot(jnp.where(same & (r >= c), 1.0, 0.0).astype(F32), g, HI)
    gl = _dot(jnp.where(same, 1.0, 0.0).astype(F32), g, HI)
    lane = lax.broadcasted_iota(jnp.int32, x.shape, 1)
    out = jnp.where(lane < _G_EGC, gc,
          jnp.where(lane < _G_EDEC, jnp.exp(gc),
          jnp.where(lane < _G_BETA, jnp.exp(gl - gc),
          jnp.where(lane < _G_GLAST, _sigmoid(x), jnp.exp(gl)))))
    o_ref[...] = out


def _gdn_gates(ab, a_log, dt_bias, ts=512):
    m = ab.shape[0]
    ts = min(ts, m)
    nh = a_log.shape[0]

    def lanes(p):
        row = jnp.zeros((LANES,), F32)
        for off in (_G_GC, _G_EGC, _G_EDEC, _G_GLAST):
            row = row.at[off:off + nh].set(p.astype(F32))
        return row.reshape(1, LANES)

    spec = pl.BlockSpec((ts, LANES), lambda i: (i, 0))
    prm = pl.BlockSpec((1, LANES), lambda i: (0, 0))
    return pl.pallas_call(
        functools.partial(_gdn_gates_body, ts=ts), grid=(m // ts,),
        in_specs=[spec, prm, prm], out_specs=spec,
        out_shape=jax.ShapeDtypeStruct((m, LANES), F32),
        compiler_params=_params("parallel"), name="gdn_gates",
    )(ab, lanes(a_log), lanes(dt_bias))


def _split3(x):
    hi = x.astype(BF16)
    lo = (x - hi.astype(F32)).astype(BF16)
    return hi, lo


def _dot3(a, b):
    a_hi, a_lo = _split3(a)
    b_hi, b_lo = _split3(b)
    lhs = jnp.concatenate([a_hi, a_lo, a_hi], axis=1)
    rhs = jnp.concatenate([b_hi, b_hi, b_lo], axis=0)
    return _dot(lhs, rhs)


def _gdn_intra_body(q_ref, k_ref, v_ref, gp_ref, gct_ref, u_ref, w_ref, qd_ref, kd_ref, at_ref,
                    *, ngrp, grows):
    cl = GDN_CHUNK
    h = pl.program_id(1)
    gp = gp_ref[0]
    gc_col = _lane_pick(gp, h + _G_GC)
    egc_col = _lane_pick(gp, h + _G_EGC)
    edec_col = _lane_pick(gp, h + _G_EDEC)
    beta_col = _lane_pick(gp, h + _G_BETA)
    r = lax.broadcasted_iota(jnp.int32, (grows, grows), 0)
    c = lax.broadcasted_iota(jnp.int32, (grows, grows), 1)
    same = (r // cl) == (c // cl)
    lower = same & (r >= c)
    strict = same & (r > c)
    steps = int(math.log2(cl))
    ps, sols = [], []
    for g in range(ngrp):
        rows = slice(g * grows, (g + 1) * grows)
        q, k, v = q_ref[0, rows, :], k_ref[0, rows, :], v_ref[0, rows, :]
        beta, egc = beta_col[rows], egc_col[rows]
        gc_row = gct_ref[0, 0, :, rows]
        decay = jnp.exp(jnp.where(lower, gc_col[rows] - gc_row, -jnp.inf))
        kb, vb = k * beta, v * beta
        kbf = k.astype(BF16)
        ps.append(-jnp.where(strict, _dot_nt(kb.astype(BF16), kbf) * decay, 0.0))
        sols.append(jnp.concatenate([vb, kb * egc], axis=-1))
        qd_ref[0, rows, :] = (q * egc).astype(qd_ref.dtype)
        kd_ref[0, rows, :] = (k * edec_col[rows]).astype(kd_ref.dtype)
        attn = jnp.where(lower, _dot_nt(q.astype(BF16), kbf) * decay, 0.0)
        for n in range(grows // cl):
            at_ref[0, 0, g * grows + n * cl:g * grows + (n + 1) * cl, :] = (
                attn[n * cl:(n + 1) * cl, n * cl:(n + 1) * cl].astype(at_ref.dtype))
    for s in range(steps):
        for g in range(ngrp):
            sols[g] = sols[g] + _dot3(ps[g], sols[g])
            if s + 1 < steps:
                ps[g] = _dot3(ps[g], ps[g])
    for g in range(ngrp):
        rows = slice(g * grows, (g + 1) * grows)
        u_ref[0, rows, :] = sols[g][:, :HEAD_DIM]
        w_ref[0, rows, :] = sols[g][:, HEAD_DIM:].astype(w_ref.dtype)


def _gdn_intra(qkv, gates, gct, ngrp=4, grows=256):
    bsz, seq, w3 = qkv.shape
    wd = w3 // 3
    nh = wd // HEAD_DIM
    cl = GDN_CHUNK
    grows = min(grows, seq)
    ngrp = min(ngrp, seq // grows)
    tt = ngrp * grows
    col = lambda off: pl.BlockSpec((1, tt, HEAD_DIM), lambda b, h, i: (b, i, h + off))
    tok_sd = lambda dt: jax.ShapeDtypeStruct((bsz, seq, wd), dt)
    return pl.pallas_call(
        functools.partial(_gdn_intra_body, ngrp=ngrp, grows=grows), grid=(bsz, nh, seq // tt),
        in_specs=[col(0), col(nh), col(2 * nh),
                  pl.BlockSpec((1, tt, LANES), lambda b, h, i: (b, i, 0)),
                  pl.BlockSpec((1, 1, 1, tt), lambda b, h, i: (b, h, 0, i))],
        out_specs=[col(0), col(0), col(0), col(0),
                   pl.BlockSpec((1, 1, tt, cl), lambda b, h, i: (b, h, i, 0))],
        out_shape=[tok_sd(F32), tok_sd(BF16), tok_sd(BF16), tok_sd(BF16),
                   jax.ShapeDtypeStruct((bsz, nh, seq, cl), BF16)],
        compiler_params=_params("parallel", "parallel", "parallel"), name="gdn_intra",
    )(qkv, qkv, qkv, gates, gct)


def _gdn_scan_body(gl_ref, u_ref, w_ref, qd_ref, kd_ref, at_ref, gate_ref, on_ref, o_ref,
                   state_ref, *, nb, nh):
    cl = GDN_CHUNK
    b = pl.program_id(0)
    i = pl.program_id(1)

    @pl.when(i == 0)
    def _():
        state_ref[...] = jnp.zeros_like(state_ref)

    for n in range(nb):
        rows = slice(n * cl, (n + 1) * cl)
        from_state = []
        for h in range(nh):
            sl = slice(h * HEAD_DIM, (h + 1) * HEAD_DIM)
            wq = jnp.concatenate([w_ref[0, rows, sl], qd_ref[0, rows, sl]], axis=0).astype(BF16)
            from_state.append(_dot(wq, state_ref[h].astype(BF16)))
        for h in range(nh):
            sl = slice(h * HEAD_DIM, (h + 1) * HEAD_DIM)
            v_new = u_ref[0, rows, sl] - from_state[h][:cl]
            vb = v_new.astype(BF16)
            out = from_state[h][cl:] + _dot(at_ref[0, h, rows, :].astype(BF16), vb)
            g_last = gl_ref[b, i * nb + n, h]
            state_ref[h] = state_ref[h] * g_last + _dot_tn(kd_ref[0, rows, sl].astype(BF16), vb)
            gate = gate_ref[0, rows, sl]
            nrm = out * lax.rsqrt(jnp.mean(out * out, axis=-1, keepdims=True) + EPS) * on_ref[...]
            o_ref[0, rows, sl] = (nrm * (gate * _sigmoid(gate))).astype(o_ref.dtype)


def _gdn_scan(glast, u_c, w_c, q_dec, k_dec, attn, main, o_norm, nb=4):
    bsz, seq, wd = u_c.shape
    nh = wd // HEAD_DIM
    cl = GDN_CHUNK
    nb = min(nb, seq // cl)
    tt = nb * cl
    tok = pl.BlockSpec((1, tt, wd), lambda b, i: (b, i, 0))
    return pl.pallas_call(
        functools.partial(_gdn_scan_body, nb=nb, nh=nh), grid=(bsz, seq // tt),
        in_specs=[pl.BlockSpec(memory_space=pltpu.SMEM), tok, tok, tok, tok,
                  pl.BlockSpec((1, nh, tt, cl), lambda b, i: (b, 0, i, 0)),
                  pl.BlockSpec((1, tt, wd), lambda b, i: (b, i, 3)),
                  pl.BlockSpec((1, HEAD_DIM), lambda b, i: (0, 0))],
        out_specs=tok,
        out_shape=jax.ShapeDtypeStruct((bsz, seq, wd), BF16),
        scratch_shapes=[pltpu.VMEM((nh, HEAD_DIM, HEAD_DIM), F32)],
        compiler_params=_params("parallel", "arbitrary"), name="gdn_scan",
    )(glast, u_c, w_c, q_dec, k_dec, attn, main, o_norm.reshape(1, HEAD_DIM).astype(F32))


def _gdn_mixer(main, ab, conv_w, a_log, dt_bias, o_norm):
    bsz, seq, w4 = main.shape
    nh = w4 // 4 // HEAD_DIM
    qkv = _gdn_conv(main, conv_w)
    gates = _gdn_gates(ab, a_log, dt_bias).reshape(bsz, seq, LANES)
    gct = gates[:, :, _G_GC:_G_GC + nh].transpose(0, 2, 1).reshape(bsz, nh, 1, seq)
    glast = gates[:, GDN_CHUNK - 1::GDN_CHUNK, _G_GLAST:_G_GLAST + nh]
    u_c, w_c, q_dec, k_dec, attn = _gdn_intra(qkv, gates, gct)
    return _gdn_scan(glast, u_c, w_c, q_dec, k_dec, attn, main, o_norm)


def _fox_cumf_body(x_ref, bf_ref, o_ref, carry_ref, *, ts):
    @pl.when(pl.program_id(1) == 0)
    def _():
        carry_ref[...] = jnp.zeros_like(carry_ref)

    ls = -_softplus(-(x_ref[0] + bf_ref[...]))
    r = lax.broadcasted_iota(jnp.int32, (ts, ts), 0)
    c = lax.broadcasted_iota(jnp.int32, (ts, ts), 1)
    cum = _dot(jnp.where(r >= c, 1.0, 0.0).astype(F32), ls, HI) + carry_ref[...]
    o_ref[0] = cum
    carry_ref[...] = cum[ts - 1:ts, :]


def _fox_cumf(f_logit, b_f, ts=256):
    bsz, seq, _ = f_logit.shape
    ts = min(ts, seq)
    bf = jnp.zeros((LANES,), F32).at[:b_f.shape[0]].set(b_f.astype(F32)).reshape(1, LANES)
    spec = pl.BlockSpec((1, ts, LANES), lambda b, i: (b, i, 0))
    return pl.pallas_call(
        functools.partial(_fox_cumf_body, ts=ts), grid=(bsz, seq // ts),
        in_specs=[spec, pl.BlockSpec((1, LANES), lambda b, i: (0, 0))],
        out_specs=spec, out_shape=jax.ShapeDtypeStruct((bsz, seq, LANES), F32),
        scratch_shapes=[pltpu.VMEM((1, LANES), F32)],
        compiler_params=_params("parallel", "arbitrary"), name="fox_cumf",
    )(f_logit, bf)


LOG2E = math.log2(math.e)
FOX_QSCALE = HEAD_DIM ** -0.5 * LOG2E


def _fox_attn_body(q_ref, k_ref, v_ref, cft_ref, o_ref, *scratch, tq, nblk):
    qi = pl.program_id(2)
    tb = tq // nblk
    m_refs, l_refs, acc_refs = scratch[0::3], scratch[1::3], scratch[2::3]
    for blk in range(nblk):
        m_refs[blk][...] = jnp.full_like(m_refs[blk], -jnp.inf)
        l_refs[blk][...] = jnp.zeros_like(l_refs[blk])
        acc_refs[blk][...] = jnp.zeros_like(acc_refs[blk])

    def scores(blk, kb):
        return _dot_nt(q_ref[0, blk * tb:(blk + 1) * tb, :], kb)

    def softmax_step(blk, s, ck, masked):
        m_ref, l_ref = m_refs[blk], l_refs[blk]
        s = s - ck
        if masked:
            r = lax.broadcasted_iota(jnp.int32, s.shape, 0) + blk * tb
            c = lax.broadcasted_iota(jnp.int32, s.shape, 1)
            s = jnp.where(r >= c, s, -jnp.inf)
        m_prev = m_ref[...]
        m_new = jnp.maximum(m_prev, jnp.max(s, axis=-1, keepdims=True))
        alpha = jnp.exp2(m_prev - m_new)
        p = jnp.exp2(s - m_new)
        l_ref[...] = alpha * l_ref[...] + jnp.sum(p, axis=-1, keepdims=True)
        m_ref[...] = m_new
        return p.astype(BF16), alpha

    def accumulate(blk, p, alpha, vb):
        acc_refs[blk][...] = alpha * acc_refs[blk][...] + _dot(p, vb)

    def key_tiles(first, count):
        tiles = []
        for t in range(count):
            start = pl.multiple_of((first + t) * tq, tq)
            tiles.append((k_ref[0, pl.ds(start, tq), :], v_ref[0, pl.ds(start, tq), :],
                          cft_ref[0, 0, first + t] * LOG2E))
        s_all = [[scores(blk, kb) for blk in range(nblk)] for kb, _, _ in tiles]
        for t, (_, vb, ck) in enumerate(tiles):
            for blk in range(nblk):
                p, alpha = softmax_step(blk, s_all[t][blk], ck, False)
                accumulate(blk, p, alpha, vb)

    def body(j, carry):
        key_tiles(2 * j, 2)
        return carry

    lax.fori_loop(0, qi // 2, body, 0)

    @pl.when(qi % 2 == 1)
    def _():
        key_tiles(qi - 1, 1)

    d0 = pl.multiple_of(qi * tq, tq)
    ck = cft_ref[0, 0, qi] * LOG2E
    nkeys = [(blk + 1) * tb for blk in range(nblk)]
    for blk in range(nblk):
        s_cur = scores(blk, k_ref[0, pl.ds(d0, nkeys[blk]), :])
        p, alpha = softmax_step(blk, s_cur, ck[:, :nkeys[blk]], True)
        accumulate(blk, p, alpha, v_ref[0, pl.ds(d0, nkeys[blk]), :])
        o_ref[0, blk * tb:(blk + 1) * tb, :] = (acc_refs[blk][...] / l_refs[blk][...]).astype(o_ref.dtype)


def _fox_attention(qkv, cumf_t, tq, nblk=4):
    bsz, seq, w3 = qkv.shape
    wd = w3 // 3
    nh = wd // HEAD_DIM
    nq = seq // tq
    tb = tq // nblk
    kv = lambda off: pl.BlockSpec((1, seq, HEAD_DIM), lambda b, h, qi: (b, 0, h + off))
    scratch = []
    for _ in range(nblk):
        scratch += [pltpu.VMEM((tb, 1), F32), pltpu.VMEM((tb, 1), F32), pltpu.VMEM((tb, HEAD_DIM), F32)]
    return pl.pallas_call(
        functools.partial(_fox_attn_body, tq=tq, nblk=nblk), grid=(bsz, nh, nq),
        in_specs=[pl.BlockSpec((1, tq, HEAD_DIM), lambda b, h, qi: (b, qi, h)),
                  kv(nh), kv(2 * nh),
                  pl.BlockSpec((1, 1, nq, 1, tq), lambda b, h, qi: (b, h, 0, 0, 0))],
        out_specs=pl.BlockSpec((1, tq, HEAD_DIM), lambda b, h, qi: (b, qi, h)),
        out_shape=jax.ShapeDtypeStruct((bsz, seq, wd), BF16),
        scratch_shapes=scratch,
        compiler_params=_params("parallel", "parallel", "arbitrary"),
        name="fox_attention",
    )(qkv, qkv, qkv, cumf_t)


def _fox_mixer(qkv, f_logit, b_f, tq=1024):
    bsz, seq, w3 = qkv.shape
    nh = w3 // 3 // HEAD_DIM
    tq = min(tq, seq)
    cumf = _fox_cumf(f_logit, b_f)
    cumf_t = cumf[:, :, :nh].transpose(0, 2, 1).reshape(bsz, nh, seq // tq, 1, tq)
    return _fox_attention(qkv, cumf_t, tq)


def _pad_cols(w, groups):
    out = jnp.zeros((w.shape[0], LANES), w.dtype)
    for off in groups:
        out = out.at[:, off:off + w.shape[1]].set(w)
    return out


def kernel(x, mem, mem_norm, w_mem_kv, norm1, w_out, norm2, w_up, w_down, norm_f,
           s5_w_in, s5_lam_re, s5_lam_im, s5_log_dt, s5_b_re, s5_b_im, s5_c_re, s5_c_im,
           s5_d_skip, s5_w_glu, s5_b_glu,
           gdn_w_in, gdn_conv_w, gdn_a_log, gdn_dt_bias, gdn_o_norm,
           fox_w_in, fox_b_f):
    bsz, seq, d = x.shape
    m = bsz * seq
    depth = norm1.shape[0]
    wd = d - MEM_WIDTH
    nh = wd // HEAD_DIM
    mlen = mem.shape[1]

    mem_a = _rmsnorm(mem.reshape(bsz * mlen, d), mem_norm)
    mkv = _matmul(mem_a, w_mem_kv.astype(BF16)[None], out_dtype=BF16, tn=512)
    mem_k = mkv[:, :MEM_WIDTH].reshape(bsz, mlen, MEM_WIDTH)
    mem_v = mkv[:, MEM_WIDTH:].reshape(bsz, mlen, MEM_WIDTH)

    w_out_b, w_down_b = w_out.astype(BF16), w_down.astype(BF16)
    s5_w_in_b, s5_w_glu_b = s5_w_in.astype(BF16), s5_w_glu.astype(BF16)

    h = x.reshape(m, d)
    a = _rmsnorm(h, norm1[0])
    out = None
    for i in range(depth):
        kind, j = i % 3, i // 3
        if kind == 0:
            chunk = min(S5_CHUNK, seq)
            u = _matmul(_to_position_major(a, chunk), s5_w_in_b, layer=j, n=wd, out_dtype=F32, tn=768)
            mq = _matmul(a, s5_w_in_b, layer=j, col0=wd // MEM_WIDTH, n=MEM_WIDTH, out_dtype=BF16,
                         tn=MEM_WIDTH)
            mix = _s5_mixer(u, s5_lam_re[j], s5_lam_im[j], s5_log_dt[j], s5_b_re[j], s5_b_im[j],
                            s5_c_re[j], s5_c_im[j], s5_d_skip[j], s5_w_glu_b, j, s5_b_glu[j],
                            bsz, chunk)
            mix = _from_position_major(mix, chunk)
        elif kind == 1:
            w_in = gdn_w_in[j]
            w_ab = (_pad_cols(w_in[:, 4 * wd:4 * wd + nh], (_G_GC, _G_EGC, _G_EDEC, _G_GLAST))
                    + _pad_cols(w_in[:, 4 * wd + nh:4 * wd + 2 * nh], (_G_BETA,)))
            main = _matmul_wcast(a, gdn_w_in, layer=j, n=4 * wd, out_dtype=F32, tn=768)
            ab = _matmul(a, w_ab.astype(BF16)[None], out_dtype=F32)
            mq = _matmul(a, w_in[:, -MEM_WIDTH:].astype(BF16)[None], out_dtype=BF16, tn=MEM_WIDTH)
            main = main.reshape(bsz, seq, 4 * wd)
            mix = _gdn_mixer(main, ab, gdn_conv_w[j], gdn_a_log[j], gdn_dt_bias[j],
                             gdn_o_norm[j]).reshape(m, wd)
        else:
            w_in = fox_w_in[j]
            qscale = jnp.concatenate([jnp.full((wd,), FOX_QSCALE, F32), jnp.ones((2 * wd,), F32)])
            qkv = _matmul_wcast(a, fox_w_in, layer=j, n=3 * wd, out_dtype=BF16, act="scale",
                                bias=qscale, tn=768)
            w_f = _pad_cols(w_in[:, 3 * wd:3 * wd + nh], (0,))
            fl = _matmul(a, w_f.astype(BF16)[None], out_dtype=F32)
            mq = _matmul(a, w_in[:, -MEM_WIDTH:].astype(BF16)[None], out_dtype=BF16, tn=MEM_WIDTH)
            mix = _fox_mixer(qkv.reshape(bsz, seq, 3 * wd), fl.reshape(bsz, seq, LANES),
                             fox_b_f[j]).reshape(m, wd)
        read = _mem_attention(mq.reshape(bsz, seq, MEM_WIDTH), mem_k, mem_v).reshape(m, MEM_WIDTH)
        h, a = _matmul_cat_norm(mix, read, w_out_b, i, h, norm2[i])
        up = _matmul_wcast(a, w_up, layer=i, out_dtype=BF16, act="relu2")
        if i + 1 < depth:
            h, a = _matmul_res_norm(up, w_down_b, i, h, norm1[i + 1], norm_dtype=BF16, keep_h=True)
        else:
            _, out = _matmul_res_norm(up, w_down_b, i, h, norm_f, norm_dtype=x.dtype, keep_h=False)
    return out.reshape(bsz, seq, d)
```

```python
import functools
import math

import jax
import jax.numpy as jnp
from jax import lax
from jax.experimental import pallas as pl
from jax.experimental.pallas import tpu as pltpu

F32 = jnp.float32
BF16 = jnp.bfloat16
EPS = 1e-6
HEAD_DIM = 128
LANES = 128
MEM_HEADS = 4
MEM_WIDTH = MEM_HEADS * HEAD_DIM
S5_GROUP = 16
S5_STATE = 64
S5_CHUNK = 64
GDN_CHUNK = 64
GDN_CONV = 4
VMEM_LIMIT_BYTES = 48 * 1024 * 1024
S5_SCAN_VMEM_BYTES = 56 * 1024 * 1024
HI = lax.Precision.HIGHEST


def _params(*sem):
    return pltpu.CompilerParams(dimension_semantics=sem, vmem_limit_bytes=VMEM_LIMIT_BYTES)


def _dot(a, b, precision=None):
    return jnp.dot(a, b, preferred_element_type=F32, precision=precision)


def _dot_nt(a, b, precision=None):
    return lax.dot_general(a, b, (((1,), (1,)), ((), ())), preferred_element_type=F32,
                           precision=precision)


def _dot_tn(a, b, precision=None):
    return lax.dot_general(a, b, (((0,), (0,)), ((), ())), preferred_element_type=F32,
                           precision=precision)


def _softplus(x):
    return jnp.maximum(x, 0.0) + jnp.log1p(jnp.exp(-jnp.abs(x)))


def _sigmoid(x):
    return 1.0 / (1.0 + jnp.exp(-x))


def _lane_pick(x, lane):
    ids = lax.broadcasted_iota(jnp.int32, x.shape, 1)
    return jnp.sum(jnp.where(ids == lane, x, 0.0), axis=-1, keepdims=True)


def _rmsnorm_body(x_ref, g_ref, o_ref):
    x = x_ref[...].astype(F32)
    ms = jnp.mean(x * x, axis=-1, keepdims=True)
    o_ref[...] = (x * lax.rsqrt(ms + EPS) * g_ref[...]).astype(o_ref.dtype)


def _rmsnorm(x2d, gain, out_dtype=BF16, tm=512):
    m, d = x2d.shape
    tm = min(tm, m)
    return pl.pallas_call(
        _rmsnorm_body, grid=(m // tm,),
        in_specs=[pl.BlockSpec((tm, d), lambda i: (i, 0)),
                  pl.BlockSpec((1, d), lambda i: (0, 0))],
        out_specs=pl.BlockSpec((tm, d), lambda i: (i, 0)),
        out_shape=jax.ShapeDtypeStruct((m, d), out_dtype),
        compiler_params=_params("parallel"), name="rmsnorm",
    )(x2d, gain.reshape(1, d).astype(F32))


def _mm_body(a_ref, w_ref, *refs, glu):
    y = _dot(a_ref[...], w_ref[...])
    if glu:
        z_ref, b_ref = refs[0], refs[1]
        y = z_ref[...].astype(F32) * _sigmoid(y + b_ref[...])
    o_ref = refs[-1]
    o_ref[...] = y.astype(o_ref.dtype)


def _matmul(a, w, *, out_dtype, layer=0, col0=0, n=None, glu_z=None, glu_bias=None, tm=1024, tn=1024):
    m, kdim = a.shape
    n = w.shape[2] if n is None else n
    tm, tn = min(tm, m), min(tn, n)
    assert m % tm == 0 and n % tn == 0 and w.shape[1] == kdim, (a.shape, w.shape, tm, tn)
    tile = pl.BlockSpec((tm, tn), lambda i, j: (i, j))
    in_specs = [pl.BlockSpec((tm, kdim), lambda i, j: (i, 0)),
                pl.BlockSpec((None, kdim, tn), lambda i, j: (layer, 0, j + col0))]
    args = [a, w]
    if glu_z is not None:
        in_specs += [tile, pl.BlockSpec((1, tn), lambda i, j: (0, j))]
        args += [glu_z, glu_bias.reshape(1, n).astype(F32)]
    return pl.pallas_call(
        functools.partial(_mm_body, glu=glu_z is not None), grid=(m // tm, n // tn),
        in_specs=in_specs, out_specs=tile,
        out_shape=jax.ShapeDtypeStruct((m, n), out_dtype),
        compiler_params=_params("parallel", "parallel"), name="matmul",
    )(*args)


def _mm_wcast_body(a_ref, w_ref, *refs, act):
    b_ref = refs[0] if act == "scale" else None
    o_ref, wb_ref = refs[-2], refs[-1]

    @pl.when(pl.program_id(1) == 0)
    def _():
        wb_ref[...] = w_ref[...].astype(wb_ref.dtype)

    y = _dot(a_ref[...], wb_ref[...])
    if act == "relu2":
        y = jnp.square(jnp.maximum(y, 0.0))
    elif act == "scale":
        y = y * b_ref[...]
    o_ref[...] = y.astype(o_ref.dtype)


def _matmul_wcast(a, w, *, out_dtype, layer=0, n=None, act=None, bias=None, tm=1024, tn=1024):
    m, kdim = a.shape
    n = w.shape[2] if n is None else n
    tm, tn = min(tm, m), min(tn, n)
    assert m % tm == 0 and n % tn == 0 and w.shape[1] == kdim
    in_specs = [pl.BlockSpec((tm, kdim), lambda j, i: (i, 0)),
                pl.BlockSpec((None, kdim, tn), lambda j, i: (layer, 0, j))]
    args = [a, w]
    if act == "scale":
        in_specs.append(pl.BlockSpec((1, tn), lambda j, i: (0, j)))
        args.append(bias.reshape(1, n).astype(F32))
    return pl.pallas_call(
        functools.partial(_mm_wcast_body, act=act), grid=(n // tn, m // tm),
        in_specs=in_specs,
        out_specs=pl.BlockSpec((tm, tn), lambda j, i: (i, j)),
        out_shape=jax.ShapeDtypeStruct((m, n), out_dtype),
        scratch_shapes=[pltpu.VMEM((kdim, tn), BF16)],
        compiler_params=_params("parallel", "arbitrary"), name="matmul_wcast",
    )(*args)


def _norm_rows(y, gain):
    return y * lax.rsqrt(jnp.mean(y * y, axis=-1, keepdims=True) + EPS) * gain


def _mm_res_norm_body(a_ref, w_ref, res_ref, g_ref, *refs, nk, keep_h):
    acc_ref, norm_ref = refs if keep_h else refs[::-1]
    k = pl.program_id(1)

    @pl.when(k == 0)
    def _():
        acc_ref[...] = res_ref[...] + _dot(a_ref[...], w_ref[...])

    @pl.when(k > 0)
    def _():
        acc_ref[...] += _dot(a_ref[...], w_ref[...])

    @pl.when(k == nk - 1)
    def _():
        norm_ref[...] = _norm_rows(acc_ref[...], g_ref[...]).astype(norm_ref.dtype)


def _matmul_res_norm(a, w, layer, res, gain, *, norm_dtype, keep_h, tm=512, tk=2048):
    m, kdim = a.shape
    n = w.shape[2]
    tm, tk = min(tm, m), min(tk, kdim)
    assert m % tm == 0 and kdim % tk == 0
    nk = kdim // tk
    row = pl.BlockSpec((tm, n), lambda i, k: (i, 0))
    out_specs = [row] * (2 if keep_h else 1)
    out_shape = ([jax.ShapeDtypeStruct((m, n), F32)] if keep_h else []) + [
        jax.ShapeDtypeStruct((m, n), norm_dtype)]
    outs = pl.pallas_call(
        functools.partial(_mm_res_norm_body, nk=nk, keep_h=keep_h), grid=(m // tm, nk),
        in_specs=[pl.BlockSpec((tm, tk), lambda i, k: (i, k)),
                  pl.BlockSpec((None, tk, n), lambda i, k: (layer, k, 0)),
                  row, pl.BlockSpec((1, n), lambda i, k: (0, 0))],
        out_specs=out_specs, out_shape=out_shape,
        scratch_shapes=[] if keep_h else [pltpu.VMEM((tm, n), F32)],
        compiler_params=_params("parallel", "arbitrary"), name="matmul_res_norm",
    )(a, w, res, gain.reshape(1, n).astype(F32))
    return outs if keep_h else (None, outs[0])


def _mm_cat_norm_body(a1_ref, a2_ref, w1_ref, w2_ref, res_ref, g_ref, h_ref, n_ref):
    y = _dot(a1_ref[...], w1_ref[...]) + _dot(a2_ref[...], w2_ref[...]) + res_ref[...]
    h_ref[...] = y
    n_ref[...] = _norm_rows(y, g_ref[...]).astype(n_ref.dtype)


def _matmul_cat_norm(a1, a2, w, layer, res, gain, tm=512):
    m, k1 = a1.shape
    k2 = a2.shape[1]
    n = w.shape[2]
    tm = min(tm, m)
    assert m % tm == 0 and w.shape[1] == k1 + k2 and k1 % k2 == 0
    row = pl.BlockSpec((tm, n), lambda i: (i, 0))
    return pl.pallas_call(
        _mm_cat_norm_body, grid=(m // tm,),
        in_specs=[pl.BlockSpec((tm, k1), lambda i: (i, 0)),
                  pl.BlockSpec((tm, k2), lambda i: (i, 0)),
                  pl.BlockSpec((None, k1, n), lambda i: (layer, 0, 0)),
                  pl.BlockSpec((None, k2, n), lambda i: (layer, k1 // k2, 0)),
                  row, pl.BlockSpec((1, n), lambda i: (0, 0))],
        out_specs=[row, row],
        out_shape=[jax.ShapeDtypeStruct((m, n), F32), jax.ShapeDtypeStruct((m, n), BF16)],
        compiler_params=_params("parallel"), name="matmul_cat_norm",
    )(a1, a2, w, w, res, gain.reshape(1, n).astype(F32))


def _mem_attn_body(q_ref, k_ref, v_ref, o_ref):
    scale = HEAD_DIM ** -0.5
    for h in range(MEM_HEADS):
        sl = slice(h * HEAD_DIM, (h + 1) * HEAD_DIM)
        logits = _dot_nt(q_ref[0, :, sl], k_ref[0, :, sl]) * scale
        mx = jnp.max(logits, axis=-1, keepdims=True)
        e = jnp.exp(logits - mx)
        p = e / jnp.sum(e, axis=-1, keepdims=True)
        o_ref[0, :, sl] = _dot(p.astype(BF16), v_ref[0, :, sl]).astype(o_ref.dtype)


def _mem_attention(q, mem_k, mem_v, ts=1024):
    bsz, seq, _ = q.shape
    mlen = mem_k.shape[1]
    ts = min(ts, seq)
    return pl.pallas_call(
        _mem_attn_body, grid=(bsz, seq // ts),
        in_specs=[pl.BlockSpec((1, ts, MEM_WIDTH), lambda b, i: (b, i, 0)),
                  pl.BlockSpec((1, mlen, MEM_WIDTH), lambda b, i: (b, 0, 0)),
                  pl.BlockSpec((1, mlen, MEM_WIDTH), lambda b, i: (b, 0, 0))],
        out_specs=pl.BlockSpec((1, ts, MEM_WIDTH), lambda b, i: (b, i, 0)),
        out_shape=jax.ShapeDtypeStruct((bsz, seq, MEM_WIDTH), BF16),
        compiler_params=_params("parallel", "parallel"), name="mem_attention",
    )(q, mem_k, mem_v)


def _s5_build_body(lre_ref, lim_ref, ldt_ref, cre_ref, cim_ref, btr_ref, bti_ref,
                   tt_ref, wbr_ref, wbi_ref, car_ref, cai_ref, al_ref, r_ref, *, chunk):
    grp, pst = S5_GROUP, S5_STATE
    l_re, l_im = lre_ref[0], lim_ref[0]
    dt = jnp.exp(ldt_ref[0])
    x_re, x_im = l_re * dt, l_im * dt
    a_mag = jnp.exp(x_re)
    a_re, a_im = a_mag * jnp.cos(x_im), a_mag * jnp.sin(x_im)
    den = l_re * l_re + l_im * l_im
    z_re = ((a_re - 1.0) * l_re + a_im * l_im) / den
    z_im = (a_im * l_re - (a_re - 1.0) * l_im) / den
    bt_re, bt_im = btr_ref[0], bti_ref[0]
    bb_re = z_re * bt_re - z_im * bt_im
    bb_im = z_re * bt_im + z_im * bt_re
    c_re, c_im = cre_ref[0], cim_ref[0]

    sub = 8
    nhi = chunk // sub

    def closed_form(t):
        mag = jnp.exp(x_re[None] * t)
        ang = x_im[None] * t
        return mag * jnp.cos(ang), mag * jnp.sin(ang)

    def powers(reverse):
        a = lax.broadcasted_iota(jnp.int32, (nhi, 1, pst), 0)
        b = lax.broadcasted_iota(jnp.int32, (sub, 1, pst), 0)
        if reverse:
            a, b = nhi - 1 - a, sub - 1 - b
        hi_re, hi_im = closed_form((a * sub).astype(F32))
        lo_re, lo_im = closed_form(b.astype(F32))
        p_re = hi_re[:, None] * lo_re[None] - hi_im[:, None] * lo_im[None]
        p_im = hi_re[:, None] * lo_im[None] + hi_im[:, None] * lo_re[None]
        return p_re.reshape(chunk, 1, pst), p_im.reshape(chunk, 1, pst)

    def times(m_re, m_im, p_re, p_im):
        o_re = m_re[None] * p_re - m_im[None] * p_im
        o_im = m_re[None] * p_im + m_im[None] * p_re
        return o_re.reshape(chunk * grp, pst), o_im.reshape(chunk * grp, pst)

    p0_re, p0_im = powers(False)
    p1_re = p0_re * a_re[None] - p0_im * a_im[None]
    p1_im = p0_re * a_im[None] + p0_im * a_re[None]
    ca_re, ca_im = times(c_re, c_im, p1_re, p1_im)
    car_ref[0] = ca_re.astype(car_ref.dtype)
    cai_ref[0] = ca_im.astype(cai_ref.dtype)
    pr_re, pr_im = powers(True)
    wb_re, wb_im = times(bb_re, bb_im, pr_re, pr_im)
    wbr_ref[0] = wb_re.astype(wbr_ref.dtype)
    wbi_ref[0] = wb_im.astype(wbi_ref.dtype)
    lf = float(chunk)
    magl = jnp.exp(x_re * lf)
    al_ref[0] = jnp.concatenate([magl * jnp.cos(x_im * lf), magl * jnp.sin(x_im * lf)], axis=0)
    ce_re, ce_im = times(c_re, c_im, p0_re, p0_im)
    r_ref[...] = _dot_nt(bb_re, ce_re, HI) - _dot_nt(bb_im, ce_im, HI)
    width = chunk * grp
    lane = lax.broadcasted_iota(jnp.int32, (grp, width), 1)
    for j in range(chunk):
        r = r_ref[...]
        if j:
            r = jnp.where(lane >= j * grp, pltpu.roll(r, j * grp, axis=1), 0.0)
        tt_ref[0, j * grp:(j + 1) * grp, :] = r.astype(tt_ref.dtype)


def _s5_build(lam_re, lam_im, log_dt, b_re, b_im, c_re, c_im, chunk):
    ng, pst = lam_re.shape
    grp = S5_GROUP
    width = chunk * grp
    row = lambda x: x.astype(F32).reshape(ng, 1, pst)
    ldt = jnp.broadcast_to(log_dt.astype(F32)[:, None, None], (ng, 1, pst))
    args = (row(lam_re), row(lam_im), ldt, c_re.astype(F32), c_im.astype(F32),
            jnp.swapaxes(b_re.astype(F32), 1, 2), jnp.swapaxes(b_im.astype(F32), 1, 2))
    spec_row = pl.BlockSpec((1, 1, pst), lambda g: (g, 0, 0))
    spec_gp = pl.BlockSpec((1, grp, pst), lambda g: (g, 0, 0))
    spec_w = pl.BlockSpec((1, width, pst), lambda g: (g, 0, 0))
    return pl.pallas_call(
        functools.partial(_s5_build_body, chunk=chunk), grid=(ng,),
        in_specs=[spec_row, spec_row, spec_row, spec_gp, spec_gp, spec_gp, spec_gp],
        out_specs=[pl.BlockSpec((1, width, width), lambda g: (g, 0, 0)),
                   spec_w, spec_w, spec_w, spec_w,
                   pl.BlockSpec((1, 2, pst), lambda g: (g, 0, 0))],
        out_shape=[jax.ShapeDtypeStruct((ng, width, width), BF16),
                   jax.ShapeDtypeStruct((ng, width, pst), BF16),
                   jax.ShapeDtypeStruct((ng, width, pst), BF16),
                   jax.ShapeDtypeStruct((ng, width, pst), BF16),
                   jax.ShapeDtypeStruct((ng, width, pst), BF16),
                   jax.ShapeDtypeStruct((ng, 2, pst), F32)],
        scratch_shapes=[pltpu.VMEM((grp, width), F32)],
        compiler_params=_params("parallel"), name="s5_build",
    )(*args)


S5_OCT = LANES // S5_GROUP


S5_GPS = 2


def _s5_scan_body(x_ref, tt_ref, wbr_ref, wbi_ref, car_ref, cai_ref, al_ref, d_ref, z_ref,
                  acc_ref, xw_ref, ug_ref, sre_ref, sim_ref, *, bsz, nc, chunk):
    grp = S5_GROUP
    step_id = pl.program_id(1)
    rows = bsz * nc
    groups = [step_id * S5_GPS + i for i in range(S5_GPS)]
    lane_grp = lax.broadcasted_iota(jnp.int32, (rows // 2, LANES), 1) // grp

    def shift(to_grp, from_grp):
        return lax.rem((to_grp - from_grp) * grp + LANES, LANES)

    def words(t):
        return pltpu.bitcast(t.astype(BF16), jnp.uint32)

    @pl.when(step_id == 0)
    def _():
        acc_ref[...] = jnp.zeros_like(acc_ref)
        for j in range(chunk):
            xw_ref[j] = words(x_ref[j])

    for i, g in enumerate(groups):
        for col in range(chunk // S5_OCT):
            packed = jnp.zeros((rows // 2, LANES), jnp.uint32)
            for jj in range(S5_OCT):
                rolled = pltpu.roll(xw_ref[col * S5_OCT + jj], shift(jj, g), axis=1)
                packed = jnp.where(lane_grp == jj, rolled, packed)
            ug_ref[i, :, col * LANES:(col + 1) * LANES] = pltpu.bitcast(packed, BF16)

    for i in range(S5_GPS):
        sre_ref[i] = _dot(ug_ref[i], wbr_ref[i])
        sim_ref[i] = _dot(ug_ref[i], wbi_ref[i])

    def step(c, carry):
        out = []
        for i in range(S5_GPS):
            a_re, a_im = al_ref[i, 0:1, :], al_ref[i, 1:2, :]
            for b in range(bsz):
                h_re, h_im = carry[i * bsz + b]
                r = b * nc + c
                loc_re, loc_im = sre_ref[i, pl.ds(r, 1), :], sim_ref[i, pl.ds(r, 1), :]
                sre_ref[i, pl.ds(r, 1), :] = h_re
                sim_ref[i, pl.ds(r, 1), :] = h_im
                out.append((a_re * h_re - a_im * h_im + loc_re, a_re * h_im + a_im * h_re + loc_im))
        return tuple(out)

    zero = jnp.zeros((1, S5_STATE), F32)
    lax.fori_loop(0, nc, step, tuple((zero, zero) for _ in range(S5_GPS * bsz)))
    ys = []
    for i in range(S5_GPS):
        y = _dot(ug_ref[i], tt_ref[i])
        y = y + _dot_nt(sre_ref[i].astype(BF16), car_ref[i])
        ys.append(y - _dot_nt(sim_ref[i].astype(BF16), cai_ref[i]))

    for k in range(chunk):
        col = k // S5_OCT
        merged = acc_ref[k]
        for g, y in zip(groups, ys):
            rolled = pltpu.roll(words(y[:, col * LANES:(col + 1) * LANES]), shift(g, k % S5_OCT), axis=1)
            merged = jnp.where(lane_grp == g, rolled, merged)
        acc_ref[k] = merged

    @pl.when(step_id == S5_OCT // S5_GPS - 1)
    def _():
        for k in range(chunk):
            yk = pltpu.bitcast(acc_ref[k], BF16).astype(F32)
            z_ref[k] = jax.nn.gelu(yk + d_ref[...] * x_ref[k]).astype(z_ref.dtype)


def _s5_scan(x, tables, d_skip, bsz):
    tt, wbr, wbi, car, cai, al = tables
    chunk, rows, width = x.shape
    pst = S5_STATE
    tw = chunk * S5_GROUP
    steps = S5_OCT // S5_GPS
    tab = lambda o, t: (o * steps + t, 0, 0)
    spec_w = pl.BlockSpec((S5_GPS, tw, pst), tab)
    blk = pl.BlockSpec((chunk, rows, LANES), lambda o, t: (0, 0, o))
    words = pltpu.VMEM((chunk, rows // 2, LANES), jnp.uint32)
    return pl.pallas_call(
        functools.partial(_s5_scan_body, bsz=bsz, nc=rows // bsz, chunk=chunk),
        grid=(width // LANES, steps),
        in_specs=[blk, pl.BlockSpec((S5_GPS, tw, tw), tab), spec_w, spec_w, spec_w, spec_w,
                  pl.BlockSpec((S5_GPS, 2, pst), tab),
                  pl.BlockSpec((1, LANES), lambda o, t: (0, o))],
        out_specs=blk,
        out_shape=jax.ShapeDtypeStruct((chunk, rows, width), BF16),
        scratch_shapes=[words, words, pltpu.VMEM((S5_GPS, rows, tw), BF16),
                        pltpu.VMEM((S5_GPS, rows, pst), F32), pltpu.VMEM((S5_GPS, rows, pst), F32)],
        compiler_params=pltpu.CompilerParams(dimension_semantics=("parallel", "arbitrary"),
                                             vmem_limit_bytes=S5_SCAN_VMEM_BYTES),
        name="s5_scan",
    )(x, tt, wbr, wbi, car, cai, al, d_skip.reshape(1, width).astype(F32))


def _to_position_major(t, chunk):
    m, width = t.shape
    return t.reshape(m // chunk, chunk, width).transpose(1, 0, 2).reshape(m, width)


def _from_position_major(t, chunk):
    m, width = t.shape
    return t.reshape(chunk, m // chunk, width).transpose(1, 0, 2).reshape(m, width)


def _s5_mixer(u_pm, lam_re, lam_im, log_dt, b_re, b_im, c_re, c_im, d_skip, w_glu, layer, b_glu,
              bsz, chunk):
    m, width = u_pm.shape
    tables = _s5_build(lam_re, lam_im, log_dt, b_re, b_im, c_re, c_im, chunk)
    z = _s5_scan(u_pm.reshape(chunk, m // chunk, width), tables, d_skip, bsz).reshape(m, width)
    return _matmul(z, w_glu, layer=layer, out_dtype=BF16, glu_z=z, glu_bias=b_glu, tn=min(width, 768))


def _gdn_conv_body(x_ref, halo_ref, w_ref, o_ref, buf_ref, *, ts, nh):
    i = pl.program_id(1)
    j = pl.program_id(2)
    halo = halo_ref[0]
    buf_ref[0:8, :] = jnp.where(i == 0, jnp.zeros_like(halo), halo)
    buf_ref[8:8 + ts, :] = x_ref[0]
    acc = None
    for tap in range(GDN_CONV):
        off = 8 - (GDN_CONV - 1) + tap
        term = buf_ref[off:off + ts, :] * w_ref[tap:tap + 1, :]
        acc = term if acc is None else acc + term
    y = acc * _sigmoid(acc)
    qscale = jnp.where(j == 0, HEAD_DIM ** -0.5, 1.0).astype(F32)
    for h in range(nh):
        sl = slice(h * HEAD_DIM, (h + 1) * HEAD_DIM)
        yh = y[:, sl]
        nrm = yh * lax.rsqrt(jnp.sum(yh * yh, axis=-1, keepdims=True) + EPS) * qscale
        o_ref[0, :, sl] = jnp.where(j == 2, yh, nrm)


def _gdn_conv(x, conv_w, ts=256):
    bsz, seq, w4 = x.shape
    wd = w4 // 4
    w3 = 3 * wd
    ts = min(ts, seq)
    per8 = ts // 8
    return pl.pallas_call(
        functools.partial(_gdn_conv_body, ts=ts, nh=wd // HEAD_DIM),
        grid=(bsz, seq // ts, 3),
        in_specs=[pl.BlockSpec((1, ts, wd), lambda b, i, j: (b, i, j)),
                  pl.BlockSpec((1, 8, wd), lambda b, i, j: (b, jnp.maximum(i * per8 - 1, 0), j)),
                  pl.BlockSpec((GDN_CONV, wd), lambda b, i, j: (0, j))],
        out_specs=pl.BlockSpec((1, ts, wd), lambda b, i, j: (b, i, j)),
        out_shape=jax.ShapeDtypeStruct((bsz, seq, w3), F32),
        scratch_shapes=[pltpu.VMEM((ts + 8, wd), F32)],
        compiler_params=_params("parallel", "parallel", "parallel"), name="gdn_conv",
    )(x, x, conv_w.astype(F32))


_G_GC, _G_EGC, _G_EDEC, _G_BETA, _G_GLAST = 0, 16, 32, 48, 64


def _gdn_gates_body(x_ref, alog_ref, dtb_ref, o_ref, *, ts):
    x = x_ref[...]
    g = -jnp.exp(alog_ref[...]) * _softplus(x + dtb_ref[...])
    r = lax.broadcasted_iota(jnp.int32, (ts, ts), 0)
    c = lax.broadcasted_iota(jnp.int32, (ts, ts), 1)
    same = (r // GDN_CHUNK) == (c // GDN_CHUNK)
    gc = _dot(jnp.where(same & (r >= c), 1.0, 0.0).astype(F32), g, HI)
    gl = _dot(jnp.where(same, 1.0, 0.0).astype(F32), g, HI)
    lane = lax.broadcasted_iota(jnp.int32, x.shape, 1)
    out = jnp.where(lane < _G_EGC, gc,
          jnp.where(lane < _G_EDEC, jnp.exp(gc),
          jnp.where(lane < _G_BETA, jnp.exp(gl - gc),
          jnp.where(lane < _G_GLAST, _sigmoid(x), jnp.exp(gl)))))
    o_ref[...] = out


def _gdn_gates(ab, a_log, dt_bias, ts=512):
    m = ab.shape[0]
    ts = min(ts, m)
    nh = a_log.shape[0]

    def lanes(p):
        row = jnp.zeros((LANES,), F32)
        for off in (_G_GC, _G_EGC, _G_EDEC, _G_GLAST):
            row = row.at[off:off + nh].set(p.astype(F32))
        return row.reshape(1, LANES)

    spec = pl.BlockSpec((ts, LANES), lambda i: (i, 0))
    prm = pl.BlockSpec((1, LANES), lambda i: (0, 0))
    return pl.pallas_call(
        functools.partial(_gdn_gates_body, ts=ts), grid=(m // ts,),
        in_specs=[spec, prm, prm], out_specs=spec,
        out_shape=jax.ShapeDtypeStruct((m, LANES), F32),
        compiler_params=_params("parallel"), name="gdn_gates",
    )(ab, lanes(a_log), lanes(dt_bias))


def _split3(x):
    hi = x.astype(BF16)
    lo = (x - hi.astype(F32)).astype(BF16)
    return hi, lo


def _dot3(a, b):
    a_hi, a_lo = _split3(a)
    b_hi, b_lo = _split3(b)
    lhs = jnp.concatenate([a_hi, a_lo, a_hi], axis=1)
    rhs = jnp.concatenate([b_hi, b_hi, b_lo], axis=0)
    return _dot(lhs, rhs)


def _gdn_intra_body(q_ref, k_ref, v_ref, gp_ref, gct_ref, u_ref, w_ref, qd_ref, kd_ref, at_ref,
                    *, ngrp, grows):
    cl = GDN_CHUNK
    h = pl.program_id(1)
    gp = gp_ref[0]
    gc_col = _lane_pick(gp, h + _G_GC)
    egc_col = _lane_pick(gp, h + _G_EGC)
    edec_col = _lane_pick(gp, h + _G_EDEC)
    beta_col = _lane_pick(gp, h + _G_BETA)
    r = lax.broadcasted_iota(jnp.int32, (grows, grows), 0)
    c = lax.broadcasted_iota(jnp.int32, (grows, grows), 1)
    same = (r // cl) == (c // cl)
    lower = same & (r >= c)
    strict = same & (r > c)
    steps = int(math.log2(cl))
    ps, sols = [], []
    for g in range(ngrp):
        rows = slice(g * grows, (g + 1) * grows)
        q, k, v = q_ref[0, rows, :], k_ref[0, rows, :], v_ref[0, rows, :]
        beta, egc = beta_col[rows], egc_col[rows]
        gc_row = gct_ref[0, 0, :, rows]
        decay = jnp.exp(jnp.where(lower, gc_col[rows] - gc_row, -jnp.inf))
        kb, vb = k * beta, v * beta
        kbf = k.astype(BF16)
        ps.append(-jnp.where(strict, _dot_nt(kb.astype(BF16), kbf) * decay, 0.0))
        sols.append(jnp.concatenate([vb, kb * egc], axis=-1))
        qd_ref[0, rows, :] = (q * egc).astype(qd_ref.dtype)
        kd_ref[0, rows, :] = (k * edec_col[rows]).astype(kd_ref.dtype)
        attn = jnp.where(lower, _dot_nt(q.astype(BF16), kbf) * decay, 0.0)
        for n in range(grows // cl):
            at_ref[0, 0, g * grows + n * cl:g * grows + (n + 1) * cl, :] = (
                attn[n * cl:(n + 1) * cl, n * cl:(n + 1) * cl].astype(at_ref.dtype))
    for s in range(steps):
        for g in range(ngrp):
            sols[g] = sols[g] + _dot3(ps[g], sols[g])
            if s + 1 < steps:
                ps[g] = _dot3(ps[g], ps[g])
    for g in range(ngrp):
        rows = slice(g * grows, (g + 1) * grows)
        u_ref[0, rows, :] = sols[g][:, :HEAD_DIM]
        w_ref[0, rows, :] = sols[g][:, HEAD_DIM:].astype(w_ref.dtype)


def _gdn_intra(qkv, gates, gct, ngrp=4, grows=256):
    bsz, seq, w3 = qkv.shape
    wd = w3 // 3
    nh = wd // HEAD_DIM
    cl = GDN_CHUNK
    grows = min(grows, seq)
    ngrp = min(ngrp, seq // grows)
    tt = ngrp * grows
    col = lambda off: pl.BlockSpec((1, tt, HEAD_DIM), lambda b, h, i: (b, i, h + off))
    tok_sd = lambda dt: jax.ShapeDtypeStruct((bsz, seq, wd), dt)
    return pl.pallas_call(
        functools.partial(_gdn_intra_body, ngrp=ngrp, grows=grows), grid=(bsz, nh, seq // tt),
        in_specs=[col(0), col(nh), col(2 * nh),
                  pl.BlockSpec((1, tt, LANES), lambda b, h, i: (b, i, 0)),
                  pl.BlockSpec((1, 1, 1, tt), lambda b, h, i: (b, h, 0, i))],
        out_specs=[col(0), col(0), col(0), col(0),
                   pl.BlockSpec((1, 1, tt, cl), lambda b, h, i: (b, h, i, 0))],
        out_shape=[tok_sd(F32), tok_sd(BF16), tok_sd(BF16), tok_sd(BF16),
                   jax.ShapeDtypeStruct((bsz, nh, seq, cl), BF16)],
        compiler_params=_params("parallel", "parallel", "parallel"), name="gdn_intra",
    )(qkv, qkv, qkv, gates, gct)


def _gdn_scan_body(gl_ref, u_ref, w_ref, qd_ref, kd_ref, at_ref, gate_ref, on_ref, o_ref,
                   state_ref, *, nb, nh):
    cl = GDN_CHUNK
    b = pl.program_id(0)
    i = pl.program_id(1)

    @pl.when(i == 0)
    def _():
        state_ref[...] = jnp.zeros_like(state_ref)

    for n in range(nb):
        rows = slice(n * cl, (n + 1) * cl)
        from_state = []
        for h in range(nh):
            sl = slice(h * HEAD_DIM, (h + 1) * HEAD_DIM)
            wq = jnp.concatenate([w_ref[0, rows, sl], qd_ref[0, rows, sl]], axis=0).astype(BF16)
            from_state.append(_dot(wq, state_ref[h].astype(BF16)))
        for h in range(nh):
            sl = slice(h * HEAD_DIM, (h + 1) * HEAD_DIM)
            v_new = u_ref[0, rows, sl] - from_state[h][:cl]
            vb = v_new.astype(BF16)
            out = from_state[h][cl:] + _dot(at_ref[0, h, rows, :].astype(BF16), vb)
            g_last = gl_ref[b, i * nb + n, h]
            state_ref[h] = state_ref[h] * g_last + _dot_tn(kd_ref[0, rows, sl].astype(BF16), vb)
            gate = gate_ref[0, rows, sl]
            nrm = out * lax.rsqrt(jnp.mean(out * out, axis=-1, keepdims=True) + EPS) * on_ref[...]
            o_ref[0, rows, sl] = (nrm * (gate * _sigmoid(gate))).astype(o_ref.dtype)


def _gdn_scan(glast, u_c, w_c, q_dec, k_dec, attn, main, o_norm, nb=4):
    bsz, seq, wd = u_c.shape
    nh = wd // HEAD_DIM
    cl = GDN_CHUNK
    nb = min(nb, seq // cl)
    tt = nb * cl
    tok = pl.BlockSpec((1, tt, wd), lambda b, i: (b, i, 0))
    return pl.pallas_call(
        functools.partial(_gdn_scan_body, nb=nb, nh=nh), grid=(bsz, seq // tt),
        in_specs=[pl.BlockSpec(memory_space=pltpu.SMEM), tok, tok, tok, tok,
                  pl.BlockSpec((1, nh, tt, cl), lambda b, i: (b, 0, i, 0)),
                  pl.BlockSpec((1, tt, wd), lambda b, i: (b, i, 3)),
                  pl.BlockSpec((1, HEAD_DIM), lambda b, i: (0, 0))],
        out_specs=tok,
        out_shape=jax.ShapeDtypeStruct((bsz, seq, wd), BF16),
        scratch_shapes=[pltpu.VMEM((nh, HEAD_DIM, HEAD_DIM), F32)],
        compiler_params=_params("parallel", "arbitrary"), name="gdn_scan",
    )(glast, u_c, w_c, q_dec, k_dec, attn, main, o_norm.reshape(1, HEAD_DIM).astype(F32))


def _gdn_mixer(main, ab, conv_w, a_log, dt_bias, o_norm):
    bsz, seq, w4 = main.shape
    nh = w4 // 4 // HEAD_DIM
    qkv = _gdn_conv(main, conv_w)
    gates = _gdn_gates(ab, a_log, dt_bias).reshape(bsz, seq, LANES)
    gct = gates[:, :, _G_GC:_G_GC + nh].transpose(0, 2, 1).reshape(bsz, nh, 1, seq)
    glast = gates[:, GDN_CHUNK - 1::GDN_CHUNK, _G_GLAST:_G_GLAST + nh]
    u_c, w_c, q_dec, k_dec, attn = _gdn_intra(qkv, gates, gct)
    return _gdn_scan(glast, u_c, w_c, q_dec, k_dec, attn, main, o_norm)


def _fox_cumf_body(x_ref, bf_ref, o_ref, carry_ref, *, ts):
    @pl.when(pl.program_id(1) == 0)
    def _():
        carry_ref[...] = jnp.zeros_like(carry_ref)

    ls = -_softplus(-(x_ref[0] + bf_ref[...]))
    r = lax.broadcasted_iota(jnp.int32, (ts, ts), 0)
    c = lax.broadcasted_iota(jnp.int32, (ts, ts), 1)
    cum = _dot(jnp.where(r >= c, 1.0, 0.0).astype(F32), ls, HI) + carry_ref[...]
    o_ref[0] = cum
    carry_ref[...] = cum[ts - 1:ts, :]


def _fox_cumf(f_logit, b_f, ts=256):
    bsz, seq, _ = f_logit.shape
    ts = min(ts, seq)
    bf = jnp.zeros((LANES,), F32).at[:b_f.shape[0]].set(b_f.astype(F32)).reshape(1, LANES)
    spec = pl.BlockSpec((1, ts, LANES), lambda b, i: (b, i, 0))
    return pl.pallas_call(
        functools.partial(_fox_cumf_body, ts=ts), grid=(bsz, seq // ts),
        in_specs=[spec, pl.BlockSpec((1, LANES), lambda b, i: (0, 0))],
        out_specs=spec, out_shape=jax.ShapeDtypeStruct((bsz, seq, LANES), F32),
        scratch_shapes=[pltpu.VMEM((1, LANES), F32)],
        compiler_params=_params("parallel", "arbitrary"), name="fox_cumf",
    )(f_logit, bf)


LOG2E = math.log2(math.e)
FOX_QSCALE = HEAD_DIM ** -0.5 * LOG2E


def _fox_attn_body(q_ref, k_ref, v_ref, cft_ref, o_ref, *scratch, tq, nblk):
    qi = pl.program_id(2)
    tb = tq // nblk
    m_refs, l_refs, acc_refs = scratch[0::3], scratch[1::3], scratch[2::3]
    for blk in range(nblk):
        m_refs[blk][...] = jnp.full_like(m_refs[blk], -jnp.inf)
        l_refs[blk][...] = jnp.zeros_like(l_refs[blk])
        acc_refs[blk][...] = jnp.zeros_like(acc_refs[blk])

    def scores(blk, kb):
        return _dot_nt(q_ref[0, blk * tb:(blk + 1) * tb, :], kb)

    def softmax_step(blk, s, ck, masked):
        m_ref, l_ref = m_refs[blk], l_refs[blk]
        s = s - ck
        if masked:
            r = lax.broadcasted_iota(jnp.int32, s.shape, 0) + blk * tb
            c = lax.broadcasted_iota(jnp.int32, s.shape, 1)
            s = jnp.where(r >= c, s, -jnp.inf)
        m_prev = m_ref[...]
        m_new = jnp.maximum(m_prev, jnp.max(s, axis=-1, keepdims=True))
        alpha = jnp.exp2(m_prev - m_new)
        p = jnp.exp2(s - m_new)
        l_ref[...] = alpha * l_ref[...] + jnp.sum(p, axis=-1, keepdims=True)
        m_ref[...] = m_new
        return p.astype(BF16), alpha

    def accumulate(blk, p, alpha, vb):
        acc_refs[blk][...] = alpha * acc_refs[blk][...] + _dot(p, vb)

    def key_tiles(first, count):
        tiles = []
        for t in range(count):
            start = pl.multiple_of((first + t) * tq, tq)
            tiles.append((k_ref[0, pl.ds(start, tq), :], v_ref[0, pl.ds(start, tq), :],
                          cft_ref[0, 0, first + t] * LOG2E))
        s_all = [[scores(blk, kb) for blk in range(nblk)] for kb, _, _ in tiles]
        for t, (_, vb, ck) in enumerate(tiles):
            for blk in range(nblk):
                p, alpha = softmax_step(blk, s_all[t][blk], ck, False)
                accumulate(blk, p, alpha, vb)

    def body(j, carry):
        key_tiles(2 * j, 2)
        return carry

    lax.fori_loop(0, qi // 2, body, 0)

    @pl.when(qi % 2 == 1)
    def _():
        key_tiles(qi - 1, 1)

    d0 = pl.multiple_of(qi * tq, tq)
    ck = cft_ref[0, 0, qi] * LOG2E
    nkeys = [(blk + 1) * tb for blk in range(nblk)]
    for blk in range(nblk):
        s_cur = scores(blk, k_ref[0, pl.ds(d0, nkeys[blk]), :])
        p, alpha = softmax_step(blk, s_cur, ck[:, :nkeys[blk]], True)
        accumulate(blk, p, alpha, v_ref[0, pl.ds(d0, nkeys[blk]), :])
        o_ref[0, blk * tb:(blk + 1) * tb, :] = (acc_refs[blk][...] / l_refs[blk][...]).astype(o_ref.dtype)


def _fox_attention(qkv, cumf_t, tq, nblk=4):
    bsz, seq, w3 = qkv.shape
    wd = w3 // 3
    nh = wd // HEAD_DIM
    nq = seq // tq
    tb = tq // nblk
    kv = lambda off: pl.BlockSpec((1, seq, HEAD_DIM), lambda b, h, qi: (b, 0, h + off))
    scratch = []
    for _ in range(nblk):
        scratch += [pltpu.VMEM((tb, 1), F32), pltpu.VMEM((tb, 1), F32), pltpu.VMEM((tb, HEAD_DIM), F32)]
    return pl.pallas_call(
        functools.partial(_fox_attn_body, tq=tq, nblk=nblk), grid=(bsz, nh, nq),
        in_specs=[pl.BlockSpec((1, tq, HEAD_DIM), lambda b, h, qi: (b, qi, h)),
                  kv(nh), kv(2 * nh),
                  pl.BlockSpec((1, 1, nq, 1, tq), lambda b, h, qi: (b, h, 0, 0, 0))],
        out_specs=pl.BlockSpec((1, tq, HEAD_DIM), lambda b, h, qi: (b, qi, h)),
        out_shape=jax.ShapeDtypeStruct((bsz, seq, wd), BF16),
        scratch_shapes=scratch,
        compiler_params=_params("parallel", "parallel", "arbitrary"),
        name="fox_attention",
    )(qkv, qkv, qkv, cumf_t)


def _fox_mixer(qkv, f_logit, b_f, tq=1024):
    bsz, seq, w3 = qkv.shape
    nh = w3 // 3 // HEAD_DIM
    tq = min(tq, seq)
    cumf = _fox_cumf(f_logit, b_f)
    cumf_t = cumf[:, :, :nh].transpose(0, 2, 1).reshape(bsz, nh, seq // tq, 1, tq)
    return _fox_attention(qkv, cumf_t, tq)


def _pad_cols(w, groups):
    out = jnp.zeros((w.shape[0], LANES), w.dtype)
    for off in groups:
        out = out.at[:, off:off + w.shape[1]].set(w)
    return out


def kernel(x, mem, mem_norm, w_mem_kv, norm1, w_out, norm2, w_up, w_down, norm_f,
           s5_w_in, s5_lam_re, s5_lam_im, s5_log_dt, s5_b_re, s5_b_im, s5_c_re, s5_c_im,
           s5_d_skip, s5_w_glu, s5_b_glu,
           gdn_w_in, gdn_conv_w, gdn_a_log, gdn_dt_bias, gdn_o_norm,
           fox_w_in, fox_b_f):
    bsz, seq, d = x.shape
    m = bsz * seq
    depth = norm1.shape[0]
    wd = d - MEM_WIDTH
    nh = wd // HEAD_DIM
    mlen = mem.shape[1]

    mem_a = _rmsnorm(mem.reshape(bsz * mlen, d), mem_norm)
    mkv = _matmul(mem_a, w_mem_kv.astype(BF16)[None], out_dtype=BF16, tn=512)
    mem_k = mkv[:, :MEM_WIDTH].reshape(bsz, mlen, MEM_WIDTH)
    mem_v = mkv[:, MEM_WIDTH:].reshape(bsz, mlen, MEM_WIDTH)

    w_out_b, w_down_b = w_out.astype(BF16), w_down.astype(BF16)
    s5_w_in_b, s5_w_glu_b = s5_w_in.astype(BF16), s5_w_glu.astype(BF16)

    h = x.reshape(m, d)
    a = _rmsnorm(h, norm1[0])
    out = None
    for i in range(depth):
        kind, j = i % 3, i // 3
        if kind == 0:
            chunk = min(S5_CHUNK, seq)
            u = _matmul(_to_position_major(a, chunk), s5_w_in_b, layer=j, n=wd, out_dtype=F32, tn=768)
            mq = _matmul(a, s5_w_in_b, layer=j, col0=wd // MEM_WIDTH, n=MEM_WIDTH, out_dtype=BF16,
                         tn=MEM_WIDTH)
            mix = _s5_mixer(u, s5_lam_re[j], s5_lam_im[j], s5_log_dt[j], s5_b_re[j], s5_b_im[j],
                            s5_c_re[j], s5_c_im[j], s5_d_skip[j], s5_w_glu_b, j, s5_b_glu[j],
                            bsz, chunk)
            mix = _from_position_major(mix, chunk)
        elif kind == 1:
            w_in = gdn_w_in[j]
            w_ab = (_pad_cols(w_in[:, 4 * wd:4 * wd + nh], (_G_GC, _G_EGC, _G_EDEC, _G_GLAST))
                    + _pad_cols(w_in[:, 4 * wd + nh:4 * wd + 2 * nh], (_G_BETA,)))
            main = _matmul_wcast(a, gdn_w_in, layer=j, n=4 * wd, out_dtype=F32, tn=768)
            ab = _matmul(a, w_ab.astype(BF16)[None], out_dtype=F32)
            mq = _matmul(a, w_in[:, -MEM_WIDTH:].astype(BF16)[None], out_dtype=BF16, tn=MEM_WIDTH)
            main = main.reshape(bsz, seq, 4 * wd)
            mix = _gdn_mixer(main, ab, gdn_conv_w[j], gdn_a_log[j], gdn_dt_bias[j],
                             gdn_o_norm[j]).reshape(m, wd)
        else:
            w_in = fox_w_in[j]
            qscale = jnp.concatenate([jnp.full((wd,), FOX_QSCALE, F32), jnp.ones((2 * wd,), F32)])
            qkv = _matmul_wcast(a, fox_w_in, layer=j, n=3 * wd, out_dtype=BF16, act="scale",
                                bias=qscale, tn=768)
            w_f = _pad_cols(w_in[:, 3 * wd:3 * wd + nh], (0,))
            fl = _matmul(a, w_f.astype(BF16)[None], out_dtype=F32)
            mq = _matmul(a, w_in[:, -MEM_WIDTH:].astype(BF16)[None], out_dtype=BF16, tn=MEM_WIDTH)
            mix = _fox_mixer(qkv.reshape(bsz, seq, 3 * wd), fl.reshape(bsz, seq, LANES),
                             fox_b_f[j]).reshape(m, wd)
        read = _mem_attention(mq.reshape(bsz, seq, MEM_WIDTH), mem_k, mem_v).reshape(m, MEM_WIDTH)
        h, a = _matmul_cat_norm(mix, read, w_out_b, i, h, norm2[i])
        up = _matmul_wcast(a, w_up, layer=i, out_dtype=BF16, act="relu2")
        if i + 1 < depth:
            h, a = _matmul_res_norm(up, w_down_b, i, h, norm1[i + 1], norm_dtype=BF16, keep_h=True)
        else:
            _, out = _matmul_res_norm(up, w_down_b, i, h, norm_f, norm_dtype=x.dtype, keep_h=False)
    return out.reshape(bsz, seq, d)
```

```python
import functools
import math

import jax
import jax.numpy as jnp
from jax import lax
from jax.experimental import pallas as pl
from jax.experimental.pallas import tpu as pltpu

F32 = jnp.float32
BF16 = jnp.bfloat16
EPS = 1e-6
HEAD_DIM = 128
LANES = 128
MEM_HEADS = 4
MEM_WIDTH = MEM_HEADS * HEAD_DIM
S5_GROUP = 16
S5_STATE = 64
S5_CHUNK = 64
GDN_CHUNK = 64
GDN_CONV = 4
VMEM_LIMIT_BYTES = 48 * 1024 * 1024
S5_SCAN_VMEM_BYTES = 56 * 1024 * 1024
HI = lax.Precision.HIGHEST


def _params(*sem):
    return pltpu.CompilerParams(dimension_semantics=sem, vmem_limit_bytes=VMEM_LIMIT_BYTES)


def _dot(a, b, precision=None):
    return jnp.dot(a, b, preferred_element_type=F32, precision=precision)


def _dot_nt(a, b, precision=None):
    return lax.dot_general(a, b, (((1,), (1,)), ((), ())), preferred_element_type=F32,
                           precision=precision)


def _dot_tn(a, b, precision=None):
    return lax.dot_general(a, b, (((0,), (0,)), ((), ())), preferred_element_type=F32,
                           precision=precision)


def _softplus(x):
    return jnp.maximum(x, 0.0) + jnp.log1p(jnp.exp(-jnp.abs(x)))


def _sigmoid(x):
    return 1.0 / (1.0 + jnp.exp(-x))


def _lane_pick(x, lane):
    ids = lax.broadcasted_iota(jnp.int32, x.shape, 1)
    return jnp.sum(jnp.where(ids == lane, x, 0.0), axis=-1, keepdims=True)


def _rmsnorm_body(x_ref, g_ref, o_ref):
    x = x_ref[...].astype(F32)
    ms = jnp.mean(x * x, axis=-1, keepdims=True)
    o_ref[...] = (x * lax.rsqrt(ms + EPS) * g_ref[...]).astype(o_ref.dtype)


def _rmsnorm(x2d, gain, out_dtype=BF16, tm=512):
    m, d = x2d.shape
    tm = min(tm, m)
    return pl.pallas_call(
        _rmsnorm_body, grid=(m // tm,),
        in_specs=[pl.BlockSpec((tm, d), lambda i: (i, 0)),
                  pl.BlockSpec((1, d), lambda i: (0, 0))],
        out_specs=pl.BlockSpec((tm, d), lambda i: (i, 0)),
        out_shape=jax.ShapeDtypeStruct((m, d), out_dtype),
        compiler_params=_params("parallel"), name="rmsnorm",
    )(x2d, gain.reshape(1, d).astype(F32))


def _mm_body(a_ref, w_ref, *refs, glu):
    y = _dot(a_ref[...], w_ref[...])
    if glu:
        z_ref, b_ref = refs[0], refs[1]
        y = z_ref[...].astype(F32) * _sigmoid(y + b_ref[...])
    o_ref = refs[-1]
    o_ref[...] = y.astype(o_ref.dtype)


def _matmul(a, w, *, out_dtype, layer=0, col0=0, n=None, glu_z=None, glu_bias=None, tm=1024, tn=1024):
    m, kdim = a.shape
    n = w.shape[2] if n is None else n
    tm, tn = min(tm, m), min(tn, n)
    assert m % tm == 0 and n % tn == 0 and w.shape[1] == kdim, (a.shape, w.shape, tm, tn)
    tile = pl.BlockSpec((tm, tn), lambda i, j: (i, j))
    in_specs = [pl.BlockSpec((tm, kdim), lambda i, j: (i, 0)),
                pl.BlockSpec((None, kdim, tn), lambda i, j: (layer, 0, j + col0))]
    args = [a, w]
    if glu_z is not None:
        in_specs += [tile, pl.BlockSpec((1, tn), lambda i, j: (0, j))]
        args += [glu_z, glu_bias.reshape(1, n).astype(F32)]
    return pl.pallas_call(
        functools.partial(_mm_body, glu=glu_z is not None), grid=(m // tm, n // tn),
        in_specs=in_specs, out_specs=tile,
        out_shape=jax.ShapeDtypeStruct((m, n), out_dtype),
        compiler_params=_params("parallel", "parallel"), name="matmul",
    )(*args)


def _mm_wcast_body(a_ref, w_ref, *refs, act):
    b_ref = refs[0] if act == "scale" else None
    o_ref, wb_ref = refs[-2], refs[-1]

    @pl.when(pl.program_id(1) == 0)
    def _():
        wb_ref[...] = w_ref[...].astype(wb_ref.dtype)

    y = _dot(a_ref[...], wb_ref[...])
    if act == "relu2":
        y = jnp.square(jnp.maximum(y, 0.0))
    elif act == "scale":
        y = y * b_ref[...]
    o_ref[...] = y.astype(o_ref.dtype)


def _matmul_wcast(a, w, *, out_dtype, layer=0, n=None, act=None, bias=None, tm=1024, tn=1024):
    m, kdim = a.shape
    n = w.shape[2] if n is None else n
    tm, tn = min(tm, m), min(tn, n)
    assert m % tm == 0 and n % tn == 0 and w.shape[1] == kdim
    in_specs = [pl.BlockSpec((tm, kdim), lambda j, i: (i, 0)),
                pl.BlockSpec((None, kdim, tn), lambda j, i: (layer, 0, j))]
    args = [a, w]
    if act == "scale":
        in_specs.append(pl.BlockSpec((1, tn), lambda j, i: (0, j)))
        args.append(bias.reshape(1, n).astype(F32))
    return pl.pallas_call(
        functools.partial(_mm_wcast_body, act=act), grid=(n // tn, m // tm),
        in_specs=in_specs,
        out_specs=pl.BlockSpec((tm, tn), lambda j, i: (i, j)),
        out_shape=jax.ShapeDtypeStruct((m, n), out_dtype),
        scratch_shapes=[pltpu.VMEM((kdim, tn), BF16)],
        compiler_params=_params("parallel", "arbitrary"), name="matmul_wcast",
    )(*args)


def _norm_rows(y, gain):
    return y * lax.rsqrt(jnp.mean(y * y, axis=-1, keepdims=True) + EPS) * gain


def _mm_res_norm_body(a_ref, w_ref, res_ref, g_ref, *refs, nk, keep_h):
    acc_ref, norm_ref = refs if keep_h else refs[::-1]
    k = pl.program_id(1)

    @pl.when(k == 0)
    def _():
        acc_ref[...] = res_ref[...] + _dot(a_ref[...], w_ref[...])

    @pl.when(k > 0)
    def _():
        acc_ref[...] += _dot(a_ref[...], w_ref[...])

    @pl.when(k == nk - 1)
    def _():
        norm_ref[...] = _norm_rows(acc_ref[...], g_ref[...]).astype(norm_ref.dtype)


def _matmul_res_norm(a, w, layer, res, gain, *, norm_dtype, keep_h, tm=512, tk=2048):
    m, kdim = a.shape
    n = w.shape[2]
    tm, tk = min(tm, m), min(tk, kdim)
    assert m % tm == 0 and kdim % tk == 0
    nk = kdim // tk
    row = pl.BlockSpec((tm, n), lambda i, k: (i, 0))
    out_specs = [row] * (2 if keep_h else 1)
    out_shape = ([jax.ShapeDtypeStruct((m, n), F32)] if keep_h else []) + [
        jax.ShapeDtypeStruct((m, n), norm_dtype)]
    outs = pl.pallas_call(
        functools.partial(_mm_res_norm_body, nk=nk, keep_h=keep_h), grid=(m // tm, nk),
        in_specs=[pl.BlockSpec((tm, tk), lambda i, k: (i, k)),
                  pl.BlockSpec((None, tk, n), lambda i, k: (layer, k, 0)),
                  row, pl.BlockSpec((1, n), lambda i, k: (0, 0))],
        out_specs=out_specs, out_shape=out_shape,
        scratch_shapes=[] if keep_h else [pltpu.VMEM((tm, n), F32)],
        compiler_params=_params("parallel", "arbitrary"), name="matmul_res_norm",
    )(a, w, res, gain.reshape(1, n).astype(F32))
    return outs if keep_h else (None, outs[0])


def _mm_cat_norm_body(a1_ref, a2_ref, w1_ref, w2_ref, res_ref, g_ref, h_ref, n_ref):
    y = _dot(a1_ref[...], w1_ref[...]) + _dot(a2_ref[...], w2_ref[...]) + res_ref[...]
    h_ref[...] = y
    n_ref[...] = _norm_rows(y, g_ref[...]).astype(n_ref.dtype)


def _matmul_cat_norm(a1, a2, w, layer, res, gain, tm=512):
    m, k1 = a1.shape
    k2 = a2.shape[1]
    n = w.shape[2]
    tm = min(tm, m)
    assert m % tm == 0 and w.shape[1] == k1 + k2 and k1 % k2 == 0
    row = pl.BlockSpec((tm, n), lambda i: (i, 0))
    return pl.pallas_call(
        _mm_cat_norm_body, grid=(m // tm,),
        in_specs=[pl.BlockSpec((tm, k1), lambda i: (i, 0)),
                  pl.BlockSpec((tm, k2), lambda i: (i, 0)),
                  pl.BlockSpec((None, k1, n), lambda i: (layer, 0, 0)),
                  pl.BlockSpec((None, k2, n), lambda i: (layer, k1 // k2, 0)),
                  row, pl.BlockSpec((1, n), lambda i: (0, 0))],
        out_specs=[row, row],
        out_shape=[jax.ShapeDtypeStruct((m, n), F32), jax.ShapeDtypeStruct((m, n), BF16)],
        compiler_params=_params("parallel"), name="matmul_cat_norm",
    )(a1, a2, w, w, res, gain.reshape(1, n).astype(F32))


def _mem_attn_body(q_ref, k_ref, v_ref, o_ref):
    scale = HEAD_DIM ** -0.5
    for h in range(MEM_HEADS):
        sl = slice(h * HEAD_DIM, (h + 1) * HEAD_DIM)
        logits = _dot_nt(q_ref[0, :, sl], k_ref[0, :, sl]) * scale
        mx = jnp.max(logits, axis=-1, keepdims=True)
        e = jnp.exp(logits - mx)
        p = e / jnp.sum(e, axis=-1, keepdims=True)
        o_ref[0, :, sl] = _dot(p.astype(BF16), v_ref[0, :, sl]).astype(o_ref.dtype)


def _mem_attention(q, mem_k, mem_v, ts=1024):
    bsz, seq, _ = q.shape
    mlen = mem_k.shape[1]
    ts = min(ts, seq)
    return pl.pallas_call(
        _mem_attn_body, grid=(bsz, seq // ts),
        in_specs=[pl.BlockSpec((1, ts, MEM_WIDTH), lambda b, i: (b, i, 0)),
                  pl.BlockSpec((1, mlen, MEM_WIDTH), lambda b, i: (b, 0, 0)),
                  pl.BlockSpec((1, mlen, MEM_WIDTH), lambda b, i: (b, 0, 0))],
        out_specs=pl.BlockSpec((1, ts, MEM_WIDTH), lambda b, i: (b, i, 0)),
        out_shape=jax.ShapeDtypeStruct((bsz, seq, MEM_WIDTH), BF16),
        compiler_params=_params("parallel", "parallel"), name="mem_attention",
    )(q, mem_k, mem_v)


def _s5_build_body(lre_ref, lim_ref, ldt_ref, cre_ref, cim_ref, btr_ref, bti_ref,
                   tt_ref, wbr_ref, wbi_ref, car_ref, cai_ref, al_ref, r_ref, *, chunk):
    grp, pst = S5_GROUP, S5_STATE
    l_re, l_im = lre_ref[0], lim_ref[0]
    dt = jnp.exp(ldt_ref[0])
    x_re, x_im = l_re * dt, l_im * dt
    a_mag = jnp.exp(x_re)
    a_re, a_im = a_mag * jnp.cos(x_im), a_mag * jnp.sin(x_im)
    den = l_re * l_re + l_im * l_im
    z_re = ((a_re - 1.0) * l_re + a_im * l_im) / den
    z_im = (a_im * l_re - (a_re - 1.0) * l_im) / den
    bt_re, bt_im = btr_ref[0], bti_ref[0]
    bb_re = z_re * bt_re - z_im * bt_im
    bb_im = z_re * bt_im + z_im * bt_re
    c_re, c_im = cre_ref[0], cim_ref[0]

    sub = 8
    nhi = chunk // sub

    def closed_form(t):
        mag = jnp.exp(x_re[None] * t)
        ang = x_im[None] * t
        return mag * jnp.cos(ang), mag * jnp.sin(ang)

    def powers(reverse):
        a = lax.broadcasted_iota(jnp.int32, (nhi, 1, pst), 0)
        b = lax.broadcasted_iota(jnp.int32, (sub, 1, pst), 0)
        if reverse:
            a, b = nhi - 1 - a, sub - 1 - b
        hi_re, hi_im = closed_form((a * sub).astype(F32))
        lo_re, lo_im = closed_form(b.astype(F32))
        p_re = hi_re[:, None] * lo_re[None] - hi_im[:, None] * lo_im[None]
        p_im = hi_re[:, None] * lo_im[None] + hi_im[:, None] * lo_re[None]
        return p_re.reshape(chunk, 1, pst), p_im.reshape(chunk, 1, pst)

    def times(m_re, m_im, p_re, p_im):
        o_re = m_re[None] * p_re - m_im[None] * p_im
        o_im = m_re[None] * p_im + m_im[None] * p_re
        return o_re.reshape(chunk * grp, pst), o_im.reshape(chunk * grp, pst)

    p0_re, p0_im = powers(False)
    p1_re = p0_re * a_re[None] - p0_im * a_im[None]
    p1_im = p0_re * a_im[None] + p0_im * a_re[None]
    ca_re, ca_im = times(c_re, c_im, p1_re, p1_im)
    car_ref[0] = ca_re.astype(car_ref.dtype)
    cai_ref[0] = ca_im.astype(cai_ref.dtype)
    pr_re, pr_im = powers(True)
    wb_re, wb_im = times(bb_re, bb_im, pr_re, pr_im)
    wbr_ref[0] = wb_re.astype(wbr_ref.dtype)
    wbi_ref[0] = wb_im.astype(wbi_ref.dtype)
    lf = float(chunk)
    magl = jnp.exp(x_re * lf)
    al_ref[0] = jnp.concatenate([magl * jnp.cos(x_im * lf), magl * jnp.sin(x_im * lf)], axis=0)
    ce_re, ce_im = times(c_re, c_im, p0_re, p0_im)
    r_ref[...] = _dot_nt(bb_re, ce_re, HI) - _dot_nt(bb_im, ce_im, HI)
    width = chunk * grp
    lane = lax.broadcasted_iota(jnp.int32, (grp, width), 1)
    for j in range(chunk):
        r = r_ref[...]
        if j:
            r = jnp.where(lane >= j * grp, pltpu.roll(r, j * grp, axis=1), 0.0)
        tt_ref[0, j * grp:(j + 1) * grp, :] = r.astype(tt_ref.dtype)


def _s5_build(lam_re, lam_im, log_dt, b_re, b_im, c_re, c_im, chunk):
    ng, pst = lam_re.shape
    grp = S5_GROUP
    width = chunk * grp
    row = lambda x: x.astype(F32).reshape(ng, 1, pst)
    ldt = jnp.broadcast_to(log_dt.astype(F32)[:, None, None], (ng, 1, pst))
    args = (row(lam_re), row(lam_im), ldt, c_re.astype(F32), c_im.astype(F32),
            jnp.swapaxes(b_re.astype(F32), 1, 2), jnp.swapaxes(b_im.astype(F32), 1, 2))
    spec_row = pl.BlockSpec((1, 1, pst), lambda g: (g, 0, 0))
    spec_gp = pl.BlockSpec((1, grp, pst), lambda g: (g, 0, 0))
    spec_w = pl.BlockSpec((1, width, pst), lambda g: (g, 0, 0))
    return pl.pallas_call(
        functools.partial(_s5_build_body, chunk=chunk), grid=(ng,),
        in_specs=[spec_row, spec_row, spec_row, spec_gp, spec_gp, spec_gp, spec_gp],
        out_specs=[pl.BlockSpec((1, width, width), lambda g: (g, 0, 0)),
                   spec_w, spec_w, spec_w, spec_w,
                   pl.BlockSpec((1, 2, pst), lambda g: (g, 0, 0))],
        out_shape=[jax.ShapeDtypeStruct((ng, width, width), BF16),
                   jax.ShapeDtypeStruct((ng, width, pst), BF16),
                   jax.ShapeDtypeStruct((ng, width, pst), BF16),
                   jax.ShapeDtypeStruct((ng, width, pst), BF16),
                   jax.ShapeDtypeStruct((ng, width, pst), BF16),
                   jax.ShapeDtypeStruct((ng, 2, pst), F32)],
        scratch_shapes=[pltpu.VMEM((grp, width), F32)],
        compiler_params=_params("parallel"), name="s5_build",
    )(*args)


S5_OCT = LANES // S5_GROUP


S5_GPS = 2


def _s5_scan_body(x_ref, tt_ref, wbr_ref, wbi_ref, car_ref, cai_ref, al_ref, d_ref, z_ref,
                  acc_ref, xw_ref, ug_ref, sre_ref, sim_ref, *, bsz, nc, chunk):
    grp = S5_GROUP
    step_id = pl.program_id(1)
    rows = bsz * nc
    groups = [step_id * S5_GPS + i for i in range(S5_GPS)]
    lane_grp = lax.broadcasted_iota(jnp.int32, (rows // 2, LANES), 1) // grp

    def shift(to_grp, from_grp):
        return lax.rem((to_grp - from_grp) * grp + LANES, LANES)

    def words(t):
        return pltpu.bitcast(t.astype(BF16), jnp.uint32)

    @pl.when(step_id == 0)
    def _():
        acc_ref[...] = jnp.zeros_like(acc_ref)
        for j in range(chunk):
            xw_ref[j] = words(x_ref[j])

    for i, g in enumerate(groups):
        for col in range(chunk // S5_OCT):
            packed = jnp.zeros((rows // 2, LANES), jnp.uint32)
            for jj in range(S5_OCT):
                rolled = pltpu.roll(xw_ref[col * S5_OCT + jj], shift(jj, g), axis=1)
                packed = jnp.where(lane_grp == jj, rolled, packed)
            ug_ref[i, :, col * LANES:(col + 1) * LANES] = pltpu.bitcast(packed, BF16)

    for i in range(S5_GPS):
        sre_ref[i] = _dot(ug_ref[i], wbr_ref[i])
        sim_ref[i] = _dot(ug_ref[i], wbi_ref[i])

    def step(c, carry):
        out = []
        for i in range(S5_GPS):
            a_re, a_im = al_ref[i, 0:1, :], al_ref[i, 1:2, :]
            for b in range(bsz):
                h_re, h_im = carry[i * bsz + b]
                r = b * nc + c
                loc_re, loc_im = sre_ref[i, pl.ds(r, 1), :], sim_ref[i, pl.ds(r, 1), :]
                sre_ref[i, pl.ds(r, 1), :] = h_re
                sim_ref[i, pl.ds(r, 1), :] = h_im
                out.append((a_re * h_re - a_im * h_im + loc_re, a_re * h_im + a_im * h_re + loc_im))
        return tuple(out)

    zero = jnp.zeros((1, S5_STATE), F32)
    lax.fori_loop(0, nc, step, tuple((zero, zero) for _ in range(S5_GPS * bsz)))
    ys = []
    for i in range(S5_GPS):
        y = _dot(ug_ref[i], tt_ref[i])
        y = y + _dot_nt(sre_ref[i].astype(BF16), car_ref[i])
        ys.append(y - _dot_nt(sim_ref[i].astype(BF16), cai_ref[i]))

    for k in range(chunk):
        col = k // S5_OCT
        merged = acc_ref[k]
        for g, y in zip(groups, ys):
            rolled = pltpu.roll(words(y[:, col * LANES:(col + 1) * LANES]), shift(g, k % S5_OCT), axis=1)
            merged = jnp.where(lane_grp == g, rolled, merged)
        acc_ref[k] = merged

    @pl.when(step_id == S5_OCT // S5_GPS - 1)
    def _():
        for k in range(chunk):
            yk = pltpu.bitcast(acc_ref[k], BF16).astype(F32)
            z_ref[k] = jax.nn.gelu(yk + d_ref[...] * x_ref[k]).astype(z_ref.dtype)


def _s5_scan(x, tables, d_skip, bsz):
    tt, wbr, wbi, car, cai, al = tables
    chunk, rows, width = x.shape
    pst = S5_STATE
    tw = chunk * S5_GROUP
    steps = S5_OCT // S5_GPS
    tab = lambda o, t: (o * steps + t, 0, 0)
    spec_w = pl.BlockSpec((S5_GPS, tw, pst), tab)
    blk = pl.BlockSpec((chunk, rows, LANES), lambda o, t: (0, 0, o))
    words = pltpu.VMEM((chunk, rows // 2, LANES), jnp.uint32)
    return pl.pallas_call(
        functools.partial(_s5_scan_body, bsz=bsz, nc=rows // bsz, chunk=chunk),
        grid=(width // LANES, steps),
        in_specs=[blk, pl.BlockSpec((S5_GPS, tw, tw), tab), spec_w, spec_w, spec_w, spec_w,
                  pl.BlockSpec((S5_GPS, 2, pst), tab),
                  pl.BlockSpec((1, LANES), lambda o, t: (0, o))],
        out_specs=blk,
        out_shape=jax.ShapeDtypeStruct((chunk, rows, width), BF16),
        scratch_shapes=[words, words, pltpu.VMEM((S5_GPS, rows, tw), BF16),
                        pltpu.VMEM((S5_GPS, rows, pst), F32), pltpu.VMEM((S5_GPS, rows, pst), F32)],
        compiler_params=pltpu.CompilerParams(dimension_semantics=("parallel", "arbitrary"),
                                             vmem_limit_bytes=S5_SCAN_VMEM_BYTES),
        name="s5_scan",
    )(x, tt, wbr, wbi, car, cai, al, d_skip.reshape(1, width).astype(F32))


def _to_position_major(t, chunk):
    m, width = t.shape
    return t.reshape(m // chunk, chunk, width).transpose(1, 0, 2).reshape(m, width)


def _from_position_major(t, chunk):
    m, width = t.shape
    return t.reshape(chunk, m // chunk, width).transpose(1, 0, 2).reshape(m, width)


def _s5_mixer(u_pm, lam_re, lam_im, log_dt, b_re, b_im, c_re, c_im, d_skip, w_glu, layer, b_glu,
              bsz, chunk):
    m, width = u_pm.shape
    tables = _s5_build(lam_re, lam_im, log_dt, b_re, b_im, c_re, c_im, chunk)
    z = _s5_scan(u_pm.reshape(chunk, m // chunk, width), tables, d_skip, bsz).reshape(m, width)
    return _matmul(z, w_glu, layer=layer, out_dtype=BF16, glu_z=z, glu_bias=b_glu, tn=min(width, 768))


def _gdn_conv_body(x_ref, halo_ref, w_ref, o_ref, buf_ref, *, ts, nh):
    i = pl.program_id(1)
    j = pl.program_id(2)
    halo = halo_ref[0]
    buf_ref[0:8, :] = jnp.where(i == 0, jnp.zeros_like(halo), halo)
    buf_ref[8:8 + ts, :] = x_ref[0]
    acc = None
    for tap in range(GDN_CONV):
        off = 8 - (GDN_CONV - 1) + tap
        term = buf_ref[off:off + ts, :] * w_ref[tap:tap + 1, :]
        acc = term if acc is None else acc + term
    o_ref[0] = acc * _sigmoid(acc)

    @pl.when(j < 2)
    def _():
        qscale = jnp.where(j == 0, HEAD_DIM ** -0.5, 1.0).astype(F32)
        for h in range(nh):
            sl = slice(h * HEAD_DIM, (h + 1) * HEAD_DIM)
            yh = o_ref[0, :, sl]
            o_ref[0, :, sl] = yh * lax.rsqrt(jnp.sum(yh * yh, axis=-1, keepdims=True) + EPS) * qscale


def _gdn_conv(x, conv_w, ts=512):
    bsz, seq, w4 = x.shape
    wd = w4 // 4
    w3 = 3 * wd
    ts = min(ts, seq)
    per8 = ts // 8
    return pl.pallas_call(
        functools.partial(_gdn_conv_body, ts=ts, nh=wd // HEAD_DIM),
        grid=(bsz, seq // ts, 3),
        in_specs=[pl.BlockSpec((1, ts, wd), lambda b, i, j: (b, i, j)),
                  pl.BlockSpec((1, 8, wd), lambda b, i, j: (b, jnp.maximum(i * per8 - 1, 0), j)),
                  pl.BlockSpec((GDN_CONV, wd), lambda b, i, j: (0, j))],
        out_specs=pl.BlockSpec((1, ts, wd), lambda b, i, j: (b, i, j)),
        out_shape=jax.ShapeDtypeStruct((bsz, seq, w3), F32),
        scratch_shapes=[pltpu.VMEM((ts + 8, wd), F32)],
        compiler_params=_params("parallel", "parallel", "parallel"), name="gdn_conv",
    )(x, x, conv_w.astype(F32))


_G_GC, _G_EGC, _G_EDEC, _G_BETA, _G_GLAST = 0, 16, 32, 48, 64


def _gdn_gates_body(x_ref, alog_ref, dtb_ref, o_ref, *, ts):
    x = x_ref[...]
    g = -jnp.exp(alog_ref[...]) * _softplus(x + dtb_ref[...])
    r = lax.broadcasted_iota(jnp.int32, (ts, ts), 0)
    c = lax.broadcasted_iota(jnp.int32, (ts, ts), 1)
    same = (r // GDN_CHUNK) == (c // GDN_CHUNK)
    gc = _dot(jnp.where(same & (r >= c), 1.0, 0.0).astype(F32), g, HI)
    gl = _dot(jnp.where(same, 1.0, 0.0).astype(F32), g, HI)
    lane = lax.broadcasted_iota(jnp.int32, x.shape, 1)
    out = jnp.where(lane < _G_EGC, gc,
          jnp.where(lane < _G_EDEC, jnp.exp(gc),
          jnp.where(lane < _G_BETA, jnp.exp(gl - gc),
          jnp.where(lane < _G_GLAST, _sigmoid(x), jnp.exp(gl)))))
    o_ref[...] = out


def _gdn_gates(ab, a_log, dt_bias, ts=512):
    m = ab.shape[0]
    ts = min(ts, m)
    nh = a_log.shape[0]

    def lanes(p):
        row = jnp.zeros((LANES,), F32)
        for off in (_G_GC, _G_EGC, _G_EDEC, _G_GLAST):
            row = row.at[off:off + nh].set(p.astype(F32))
        return row.reshape(1, LANES)

    spec = pl.BlockSpec((ts, LANES), lambda i: (i, 0))
    prm = pl.BlockSpec((1, LANES), lambda i: (0, 0))
    return pl.pallas_call(
        functools.partial(_gdn_gates_body, ts=ts), grid=(m // ts,),
        in_specs=[spec, prm, prm], out_specs=spec,
        out_shape=jax.ShapeDtypeStruct((m, LANES), F32),
        compiler_params=_params("parallel"), name="gdn_gates",
    )(ab, lanes(a_log), lanes(dt_bias))


def _split3(x):
    hi = x.astype(BF16)
    lo = (x - hi.astype(F32)).astype(BF16)
    return hi, lo


def _dot3(a, b):
    a_hi, a_lo = _split3(a)
    b_hi, b_lo = _split3(b)
    lhs = jnp.concatenate([a_hi, a_lo, a_hi], axis=1)
    rhs = jnp.concatenate([b_hi, b_hi, b_lo], axis=0)
    return _dot(lhs, rhs)


def _gdn_intra_body(q_ref, k_ref, v_ref, gp_ref, gct_ref, u_ref, w_ref, qd_ref, kd_ref, at_ref,
                    *, ngrp, grows):
    cl = GDN_CHUNK
    h = pl.program_id(1)
    gp = gp_ref[0]
    gc_col = _lane_pick(gp, h + _G_GC)
    egc_col = _lane_pick(gp, h + _G_EGC)
    edec_col = _lane_pick(gp, h + _G_EDEC)
    beta_col = _lane_pick(gp, h + _G_BETA)
    r = lax.broadcasted_iota(jnp.int32, (grows, grows), 0)
    c = lax.broadcasted_iota(jnp.int32, (grows, grows), 1)
    same = (r // cl) == (c // cl)
    lower = same & (r >= c)
    strict = same & (r > c)
    steps = int(math.log2(cl))
    ps, sols = [], []
    for g in range(ngrp):
        rows = slice(g * grows, (g + 1) * grows)
        q, k, v = q_ref[0, rows, :], k_ref[0, rows, :], v_ref[0, rows, :]
        beta, egc = beta_col[rows], egc_col[rows]
        gc_row = gct_ref[0, 0, :, rows]
        decay = jnp.exp(jnp.where(lower, gc_col[rows] - gc_row, -jnp.inf))
        kb, vb = k * beta, v * beta
        kbf = k.astype(BF16)
        ps.append(-jnp.where(strict, _dot_nt(kb.astype(BF16), kbf) * decay, 0.0))
        sols.append(jnp.concatenate([vb, kb * egc], axis=-1))
        qd_ref[0, rows, :] = (q * egc).astype(qd_ref.dtype)
        kd_ref[0, rows, :] = (k * edec_col[rows]).astype(kd_ref.dtype)
        attn = jnp.where(lower, _dot_nt(q.astype(BF16), kbf) * decay, 0.0)
        for n in range(grows // cl):
            at_ref[0, 0, g * grows + n * cl:g * grows + (n + 1) * cl, :] = (
                attn[n * cl:(n + 1) * cl, n * cl:(n + 1) * cl].astype(at_ref.dtype))
    for s in range(steps):
        for g in range(ngrp):
            sols[g] = sols[g] + _dot3(ps[g], sols[g])
            if s + 1 < steps:
                ps[g] = _dot3(ps[g], ps[g])
    for g in range(ngrp):
        rows = slice(g * grows, (g + 1) * grows)
        u_ref[0, rows, :] = sols[g][:, :HEAD_DIM]
        w_ref[0, rows, :] = sols[g][:, HEAD_DIM:].astype(w_ref.dtype)


def _gdn_intra(qkv, gates, gct, ngrp=4, grows=256):
    bsz, seq, w3 = qkv.shape
    wd = w3 // 3
    nh = wd // HEAD_DIM
    cl = GDN_CHUNK
    grows = min(grows, seq)
    ngrp = min(ngrp, seq // grows)
    tt = ngrp * grows
    col = lambda off: pl.BlockSpec((1, tt, HEAD_DIM), lambda b, h, i: (b, i, h + off))
    tok_sd = lambda dt: jax.ShapeDtypeStruct((bsz, seq, wd), dt)
    return pl.pallas_call(
        functools.partial(_gdn_intra_body, ngrp=ngrp, grows=grows), grid=(bsz, nh, seq // tt),
        in_specs=[col(0), col(nh), col(2 * nh),
                  pl.BlockSpec((1, tt, LANES), lambda b, h, i: (b, i, 0)),
                  pl.BlockSpec((1, 1, 1, tt), lambda b, h, i: (b, h, 0, i))],
        out_specs=[col(0), col(0), col(0), col(0),
                   pl.BlockSpec((1, 1, tt, cl), lambda b, h, i: (b, h, i, 0))],
        out_shape=[tok_sd(F32), tok_sd(BF16), tok_sd(BF16), tok_sd(BF16),
                   jax.ShapeDtypeStruct((bsz, nh, seq, cl), BF16)],
        compiler_params=_params("parallel", "parallel", "parallel"), name="gdn_intra",
    )(qkv, qkv, qkv, gates, gct)


def _gdn_scan_body(gl_ref, u_ref, w_ref, qd_ref, kd_ref, at_ref, gate_ref, on_ref, o_ref,
                   state_ref, *, nb, nh):
    cl = GDN_CHUNK
    b = pl.program_id(0)
    i = pl.program_id(1)

    @pl.when(i == 0)
    def _():
        state_ref[...] = jnp.zeros_like(state_ref)

    for n in range(nb):
        rows = slice(n * cl, (n + 1) * cl)
        from_state = []
        for h in range(nh):
            sl = slice(h * HEAD_DIM, (h + 1) * HEAD_DIM)
            wq = jnp.concatenate([w_ref[0, rows, sl], qd_ref[0, rows, sl]], axis=0).astype(BF16)
            from_state.append(_dot(wq, state_ref[h].astype(BF16)))
        for h in range(nh):
            sl = slice(h * HEAD_DIM, (h + 1) * HEAD_DIM)
            v_new = u_ref[0, rows, sl] - from_state[h][:cl]
            vb = v_new.astype(BF16)
            out = from_state[h][cl:] + _dot(at_ref[0, h, rows, :].astype(BF16), vb)
            g_last = gl_ref[b, i * nb + n, h]
            state_ref[h] = state_ref[h] * g_last + _dot_tn(kd_ref[0, rows, sl].astype(BF16), vb)
            gate = gate_ref[0, rows, sl]
            nrm = out * lax.rsqrt(jnp.mean(out * out, axis=-1, keepdims=True) + EPS) * on_ref[...]
            o_ref[0, rows, sl] = (nrm * (gate * _sigmoid(gate))).astype(o_ref.dtype)


def _gdn_scan(glast, u_c, w_c, q_dec, k_dec, attn, main, o_norm, nb=4):
    bsz, seq, wd = u_c.shape
    nh = wd // HEAD_DIM
    cl = GDN_CHUNK
    nb = min(nb, seq // cl)
    tt = nb * cl
    tok = pl.BlockSpec((1, tt, wd), lambda b, i: (b, i, 0))
    return pl.pallas_call(
        functools.partial(_gdn_scan_body, nb=nb, nh=nh), grid=(bsz, seq // tt),
        in_specs=[pl.BlockSpec(memory_space=pltpu.SMEM), tok, tok, tok, tok,
                  pl.BlockSpec((1, nh, tt, cl), lambda b, i: (b, 0, i, 0)),
                  pl.BlockSpec((1, tt, wd), lambda b, i: (b, i, 3)),
                  pl.BlockSpec((1, HEAD_DIM), lambda b, i: (0, 0))],
        out_specs=tok,
        out_shape=jax.ShapeDtypeStruct((bsz, seq, wd), BF16),
        scratch_shapes=[pltpu.VMEM((nh, HEAD_DIM, HEAD_DIM), F32)],
        compiler_params=_params("parallel", "arbitrary"), name="gdn_scan",
    )(glast, u_c, w_c, q_dec, k_dec, attn, main, o_norm.reshape(1, HEAD_DIM).astype(F32))


def _gdn_mixer(main, ab, conv_w, a_log, dt_bias, o_norm):
    bsz, seq, w4 = main.shape
    nh = w4 // 4 // HEAD_DIM
    qkv = _gdn_conv(main, conv_w)
    gates = _gdn_gates(ab, a_log, dt_bias).reshape(bsz, seq, LANES)
    gct = gates[:, :, _G_GC:_G_GC + nh].transpose(0, 2, 1).reshape(bsz, nh, 1, seq)
    glast = gates[:, GDN_CHUNK - 1::GDN_CHUNK, _G_GLAST:_G_GLAST + nh]
    u_c, w_c, q_dec, k_dec, attn = _gdn_intra(qkv, gates, gct)
    return _gdn_scan(glast, u_c, w_c, q_dec, k_dec, attn, main, o_norm)


def _fox_cumf_body(x_ref, bf_ref, o_ref, carry_ref, *, ts):
    @pl.when(pl.program_id(1) == 0)
    def _():
        carry_ref[...] = jnp.zeros_like(carry_ref)

    ls = -_softplus(-(x_ref[0] + bf_ref[...]))
    r = lax.broadcasted_iota(jnp.int32, (ts, ts), 0)
    c = lax.broadcasted_iota(jnp.int32, (ts, ts), 1)
    cum = _dot(jnp.where(r >= c, 1.0, 0.0).astype(F32), ls, HI) + carry_ref[...]
    o_ref[0] = cum
    carry_ref[...] = cum[ts - 1:ts, :]


def _fox_cumf(f_logit, b_f, ts=256):
    bsz, seq, _ = f_logit.shape
    ts = min(ts, seq)
    bf = jnp.zeros((LANES,), F32).at[:b_f.shape[0]].set(b_f.astype(F32)).reshape(1, LANES)
    spec = pl.BlockSpec((1, ts, LANES), lambda b, i: (b, i, 0))
    return pl.pallas_call(
        functools.partial(_fox_cumf_body, ts=ts), grid=(bsz, seq // ts),
        in_specs=[spec, pl.BlockSpec((1, LANES), lambda b, i: (0, 0))],
        out_specs=spec, out_shape=jax.ShapeDtypeStruct((bsz, seq, LANES), F32),
        scratch_shapes=[pltpu.VMEM((1, LANES), F32)],
        compiler_params=_params("parallel", "arbitrary"), name="fox_cumf",
    )(f_logit, bf)


LOG2E = math.log2(math.e)
FOX_QSCALE = HEAD_DIM ** -0.5 * LOG2E


def _fox_attn_body(q_ref, k_ref, v_ref, cft_ref, o_ref, *scratch, tq, nblk):
    qi = pl.program_id(2)
    tb = tq // nblk
    m_refs, l_refs, acc_refs = scratch[0::3], scratch[1::3], scratch[2::3]
    for blk in range(nblk):
        m_refs[blk][...] = jnp.full_like(m_refs[blk], -jnp.inf)
        l_refs[blk][...] = jnp.zeros_like(l_refs[blk])
        acc_refs[blk][...] = jnp.zeros_like(acc_refs[blk])

    def scores(blk, kb):
        return _dot_nt(q_ref[0, blk * tb:(blk + 1) * tb, :], kb)

    def softmax_step(blk, s, ck, masked):
        m_ref, l_ref = m_refs[blk], l_refs[blk]
        s = s - ck
        if masked:
            r = lax.broadcasted_iota(jnp.int32, s.shape, 0) + blk * tb
            c = lax.broadcasted_iota(jnp.int32, s.shape, 1)
            s = jnp.where(r >= c, s, -jnp.inf)
        m_prev = m_ref[...]
        m_new = jnp.maximum(m_prev, jnp.max(s, axis=-1, keepdims=True))
        alpha = jnp.exp2(m_prev - m_new)
        p = jnp.exp2(s - m_new)
        l_ref[...] = alpha * l_ref[...] + jnp.sum(p, axis=-1, keepdims=True)
        m_ref[...] = m_new
        return p.astype(BF16), alpha

    def accumulate(blk, p, alpha, vb):
        acc_refs[blk][...] = alpha * acc_refs[blk][...] + _dot(p, vb)

    def key_tiles(first, count):
        tiles = []
        for t in range(count):
            start = pl.multiple_of((first + t) * tq, tq)
            tiles.append((k_ref[0, pl.ds(start, tq), :], v_ref[0, pl.ds(start, tq), :],
                          cft_ref[0, 0, first + t] * LOG2E))
        s_all = [[scores(blk, kb) for blk in range(nblk)] for kb, _, _ in tiles]
        for t, (_, vb, ck) in enumerate(tiles):
            for blk in range(nblk):
                p, alpha = softmax_step(blk, s_all[t][blk], ck, False)
                accumulate(blk, p, alpha, vb)

    def body(j, carry):
        key_tiles(2 * j, 2)
        return carry

    lax.fori_loop(0, qi // 2, body, 0)

    @pl.when(qi % 2 == 1)
    def _():
        key_tiles(qi - 1, 1)

    d0 = pl.multiple_of(qi * tq, tq)
    ck = cft_ref[0, 0, qi] * LOG2E
    nkeys = [(blk + 1) * tb for blk in range(nblk)]
    for blk in range(nblk):
        s_cur = scores(blk, k_ref[0, pl.ds(d0, nkeys[blk]), :])
        p, alpha = softmax_step(blk, s_cur, ck[:, :nkeys[blk]], True)
        accumulate(blk, p, alpha, v_ref[0, pl.ds(d0, nkeys[blk]), :])
        o_ref[0, blk * tb:(blk + 1) * tb, :] = (acc_refs[blk][...] / l_refs[blk][...]).astype(o_ref.dtype)


def _fox_attention(qkv, cumf_t, tq, nblk=4):
    bsz, seq, w3 = qkv.shape
    wd = w3 // 3
    nh = wd // HEAD_DIM
    nq = seq // tq
    tb = tq // nblk
    kv = lambda off: pl.BlockSpec((1, seq, HEAD_DIM), lambda b, h, qi: (b, 0, h + off))
    scratch = []
    for _ in range(nblk):
        scratch += [pltpu.VMEM((tb, 1), F32), pltpu.VMEM((tb, 1), F32), pltpu.VMEM((tb, HEAD_DIM), F32)]
    return pl.pallas_call(
        functools.partial(_fox_attn_body, tq=tq, nblk=nblk), grid=(bsz, nh, nq),
        in_specs=[pl.BlockSpec((1, tq, HEAD_DIM), lambda b, h, qi: (b, qi, h)),
                  kv(nh), kv(2 * nh),
                  pl.BlockSpec((1, 1, nq, 1, tq), lambda b, h, qi: (b, h, 0, 0, 0))],
        out_specs=pl.BlockSpec((1, tq, HEAD_DIM), lambda b, h, qi: (b, qi, h)),
        out_shape=jax.ShapeDtypeStruct((bsz, seq, wd), BF16),
        scratch_shapes=scratch,
        compiler_params=_params("parallel", "parallel", "arbitrary"),
        name="fox_attention",
    )(qkv, qkv, qkv, cumf_t)


def _fox_mixer(qkv, f_logit, b_f, tq=1024):
    bsz, seq, w3 = qkv.shape
    nh = w3 // 3 // HEAD_DIM
    tq = min(tq, seq)
    cumf = _fox_cumf(f_logit, b_f)
    cumf_t = cumf[:, :, :nh].transpose(0, 2, 1).reshape(bsz, nh, seq // tq, 1, tq)
    return _fox_attention(qkv, cumf_t, tq)


def _pad_cols(w, groups):
    out = jnp.zeros((w.shape[0], LANES), w.dtype)
    for off in groups:
        out = out.at[:, off:off + w.shape[1]].set(w)
    return out


def kernel(x, mem, mem_norm, w_mem_kv, norm1, w_out, norm2, w_up, w_down, norm_f,
           s5_w_in, s5_lam_re, s5_lam_im, s5_log_dt, s5_b_re, s5_b_im, s5_c_re, s5_c_im,
           s5_d_skip, s5_w_glu, s5_b_glu,
           gdn_w_in, gdn_conv_w, gdn_a_log, gdn_dt_bias, gdn_o_norm,
           fox_w_in, fox_b_f):
    bsz, seq, d = x.shape
    m = bsz * seq
    depth = norm1.shape[0]
    wd = d - MEM_WIDTH
    nh = wd // HEAD_DIM
    mlen = mem.shape[1]

    mem_a = _rmsnorm(mem.reshape(bsz * mlen, d), mem_norm)
    mkv = _matmul(mem_a, w_mem_kv.astype(BF16)[None], out_dtype=BF16, tn=512)
    mem_k = mkv[:, :MEM_WIDTH].reshape(bsz, mlen, MEM_WIDTH)
    mem_v = mkv[:, MEM_WIDTH:].reshape(bsz, mlen, MEM_WIDTH)

    w_out_b, w_down_b = w_out.astype(BF16), w_down.astype(BF16)
    s5_w_in_b, s5_w_glu_b = s5_w_in.astype(BF16), s5_w_glu.astype(BF16)

    h = x.reshape(m, d)
    a = _rmsnorm(h, norm1[0])
    out = None
    for i in range(depth):
        kind, j = i % 3, i // 3
        if kind == 0:
            chunk = min(S5_CHUNK, seq)
            u = _matmul(_to_position_major(a, chunk), s5_w_in_b, layer=j, n=wd, out_dtype=F32, tn=768)
            mq = _matmul(a, s5_w_in_b, layer=j, col0=wd // MEM_WIDTH, n=MEM_WIDTH, out_dtype=BF16,
                         tn=MEM_WIDTH)
            mix = _s5_mixer(u, s5_lam_re[j], s5_lam_im[j], s5_log_dt[j], s5_b_re[j], s5_b_im[j],
                            s5_c_re[j], s5_c_im[j], s5_d_skip[j], s5_w_glu_b, j, s5_b_glu[j],
                            bsz, chunk)
            mix = _from_position_major(mix, chunk)
        elif kind == 1:
            w_in = gdn_w_in[j]
            w_ab = (_pad_cols(w_in[:, 4 * wd:4 * wd + nh], (_G_GC, _G_EGC, _G_EDEC, _G_GLAST))
                    + _pad_cols(w_in[:, 4 * wd + nh:4 * wd + 2 * nh], (_G_BETA,)))
            main = _matmul_wcast(a, gdn_w_in, layer=j, n=4 * wd, out_dtype=F32, tn=768)
            ab = _matmul(a, w_ab.astype(BF16)[None], out_dtype=F32)
            mq = _matmul(a, w_in[:, -MEM_WIDTH:].astype(BF16)[None], out_dtype=BF16, tn=MEM_WIDTH)
            main = main.reshape(bsz, seq, 4 * wd)
            mix = _gdn_mixer(main, ab, gdn_conv_w[j], gdn_a_log[j], gdn_dt_bias[j],
                             gdn_o_norm[j]).reshape(m, wd)
        else:
            w_in = fox_w_in[j]
            qscale = jnp.concatenate([jnp.full((wd,), FOX_QSCALE, F32), jnp.ones((2 * wd,), F32)])
            qkv = _matmul_wcast(a, fox_w_in, layer=j, n=3 * wd, out_dtype=BF16, act="scale",
                                bias=qscale, tn=768)
            w_f = _pad_cols(w_in[:, 3 * wd:3 * wd + nh], (0,))
            fl = _matmul(a, w_f.astype(BF16)[None], out_dtype=F32)
            mq = _matmul(a, w_in[:, -MEM_WIDTH:].astype(BF16)[None], out_dtype=BF16, tn=MEM_WIDTH)
            mix = _fox_mixer(qkv.reshape(bsz, seq, 3 * wd), fl.reshape(bsz, seq, LANES),
                             fox_b_f[j]).reshape(m, wd)
        read = _mem_attention(mq.reshape(bsz, seq, MEM_WIDTH), mem_k, mem_v).reshape(m, MEM_WIDTH)
        h, a = _matmul_cat_norm(mix, read, w_out_b, i, h, norm2[i])
        up = _matmul_wcast(a, w_up, layer=i, out_dtype=BF16, act="relu2")
        if i + 1 < depth:
            h, a = _matmul_res_norm(up, w_down_b, i, h, norm1[i + 1], norm_dtype=BF16, keep_h=True)
        else:
            _, out = _matmul_res_norm(up, w_down_b, i, h, norm_f, norm_dtype=x.dtype, keep_h=False)
    return out.reshape(bsz, seq, d)
```

```python
import functools
import math

import jax
import jax.numpy as jnp
from jax import lax
from jax.experimental import pallas as pl
from jax.experimental.pallas import tpu as pltpu

F32 = jnp.float32
BF16 = jnp.bfloat16
EPS = 1e-6
HEAD_DIM = 128
LANES = 128
MEM_HEADS = 4
MEM_WIDTH = MEM_HEADS * HEAD_DIM
S5_GROUP = 16
S5_STATE = 64
S5_CHUNK = 64
GDN_CHUNK = 64
GDN_CONV = 4
VMEM_LIMIT_BYTES = 48 * 1024 * 1024
S5_SCAN_VMEM_BYTES = 56 * 1024 * 1024
HI = lax.Precision.HIGHEST


def _params(*sem):
    return pltpu.CompilerParams(dimension_semantics=sem, vmem_limit_bytes=VMEM_LIMIT_BYTES)


def _dot(a, b, precision=None):
    return jnp.dot(a, b, preferred_element_type=F32, precision=precision)


def _dot_nt(a, b, precision=None):
    return lax.dot_general(a, b, (((1,), (1,)), ((), ())), preferred_element_type=F32,
                           precision=precision)


def _dot_tn(a, b, precision=None):
    return lax.dot_general(a, b, (((0,), (0,)), ((), ())), preferred_element_type=F32,
                           precision=precision)


def _softplus(x):
    return jnp.maximum(x, 0.0) + jnp.log1p(jnp.exp(-jnp.abs(x)))


def _sigmoid(x):
    return 1.0 / (1.0 + jnp.exp(-x))


def _lane_pick(x, lane):
    ids = lax.broadcasted_iota(jnp.int32, x.shape, 1)
    return jnp.sum(jnp.where(ids == lane, x, 0.0), axis=-1, keepdims=True)


def _rmsnorm_body(x_ref, g_ref, o_ref):
    x = x_ref[...].astype(F32)
    ms = jnp.mean(x * x, axis=-1, keepdims=True)
    o_ref[...] = (x * lax.rsqrt(ms + EPS) * g_ref[...]).astype(o_ref.dtype)


def _rmsnorm(x2d, gain, out_dtype=BF16, tm=512):
    m, d = x2d.shape
    tm = min(tm, m)
    return pl.pallas_call(
        _rmsnorm_body, grid=(m // tm,),
        in_specs=[pl.BlockSpec((tm, d), lambda i: (i, 0)),
                  pl.BlockSpec((1, d), lambda i: (0, 0))],
        out_specs=pl.BlockSpec((tm, d), lambda i: (i, 0)),
        out_shape=jax.ShapeDtypeStruct((m, d), out_dtype),
        compiler_params=_params("parallel"), name="rmsnorm",
    )(x2d, gain.reshape(1, d).astype(F32))


def _mm_body(a_ref, w_ref, *refs, glu):
    y = _dot(a_ref[...], w_ref[...])
    if glu:
        z_ref, b_ref = refs[0], refs[1]
        y = z_ref[...].astype(F32) * _sigmoid(y + b_ref[...])
    o_ref = refs[-1]
    o_ref[...] = y.astype(o_ref.dtype)


def _matmul(a, w, *, out_dtype, layer=0, col0=0, n=None, glu_z=None, glu_bias=None, tm=1024, tn=1024):
    m, kdim = a.shape
    n = w.shape[2] if n is None else n
    tm, tn = min(tm, m), min(tn, n)
    assert m % tm == 0 and n % tn == 0 and w.shape[1] == kdim, (a.shape, w.shape, tm, tn)
    tile = pl.BlockSpec((tm, tn), lambda i, j: (i, j))
    in_specs = [pl.BlockSpec((tm, kdim), lambda i, j: (i, 0)),
                pl.BlockSpec((None, kdim, tn), lambda i, j: (layer, 0, j + col0))]
    args = [a, w]
    if glu_z is not None:
        in_specs += [tile, pl.BlockSpec((1, tn), lambda i, j: (0, j))]
        args += [glu_z, glu_bias.reshape(1, n).astype(F32)]
    return pl.pallas_call(
        functools.partial(_mm_body, glu=glu_z is not None), grid=(m // tm, n // tn),
        in_specs=in_specs, out_specs=tile,
        out_shape=jax.ShapeDtypeStruct((m, n), out_dtype),
        compiler_params=_params("parallel", "parallel"), name="matmul",
    )(*args)


def _mm_wcast_body(a_ref, w_ref, *refs, act):
    b_ref = refs[0] if act == "scale" else None
    o_ref, wb_ref = refs[-2], refs[-1]

    @pl.when(pl.program_id(1) == 0)
    def _():
        wb_ref[...] = w_ref[...].astype(wb_ref.dtype)

    y = _dot(a_ref[...], wb_ref[...])
    if act == "relu2":
        y = jnp.square(jnp.maximum(y, 0.0))
    elif act == "scale":
        y = y * b_ref[...]
    o_ref[...] = y.astype(o_ref.dtype)


def _matmul_wcast(a, w, *, out_dtype, layer=0, n=None, act=None, bias=None, tm=1024, tn=1024):
    m, kdim = a.shape
    n = w.shape[2] if n is None else n
    tm, tn = min(tm, m), min(tn, n)
    assert m % tm == 0 and n % tn == 0 and w.shape[1] == kdim
    in_specs = [pl.BlockSpec((tm, kdim), lambda j, i: (i, 0)),
                pl.BlockSpec((None, kdim, tn), lambda j, i: (layer, 0, j))]
    args = [a, w]
    if act == "scale":
        in_specs.append(pl.BlockSpec((1, tn), lambda j, i: (0, j)))
        args.append(bias.reshape(1, n).astype(F32))
    return pl.pallas_call(
        functools.partial(_mm_wcast_body, act=act), grid=(n // tn, m // tm),
        in_specs=in_specs,
        out_specs=pl.BlockSpec((tm, tn), lambda j, i: (i, j)),
        out_shape=jax.ShapeDtypeStruct((m, n), out_dtype),
        scratch_shapes=[pltpu.VMEM((kdim, tn), BF16)],
        compiler_params=_params("parallel", "arbitrary"), name="matmul_wcast",
    )(*args)


def _norm_rows(y, gain):
    return y * lax.rsqrt(jnp.mean(y * y, axis=-1, keepdims=True) + EPS) * gain


def _mm_res_norm_body(a_ref, w_ref, res_ref, g_ref, *refs, nk, keep_h):
    acc_ref, norm_ref = refs if keep_h else refs[::-1]
    k = pl.program_id(1)

    @pl.when(k == 0)
    def _():
        acc_ref[...] = res_ref[...] + _dot(a_ref[...], w_ref[...])

    @pl.when(k > 0)
    def _():
        acc_ref[...] += _dot(a_ref[...], w_ref[...])

    @pl.when(k == nk - 1)
    def _():
        norm_ref[...] = _norm_rows(acc_ref[...], g_ref[...]).astype(norm_ref.dtype)


def _matmul_res_norm(a, w, layer, res, gain, *, norm_dtype, keep_h, tm=512, tk=2048):
    m, kdim = a.shape
    n = w.shape[2]
    tm, tk = min(tm, m), min(tk, kdim)
    assert m % tm == 0 and kdim % tk == 0
    nk = kdim // tk
    row = pl.BlockSpec((tm, n), lambda i, k: (i, 0))
    out_specs = [row] * (2 if keep_h else 1)
    out_shape = ([jax.ShapeDtypeStruct((m, n), F32)] if keep_h else []) + [
        jax.ShapeDtypeStruct((m, n), norm_dtype)]
    outs = pl.pallas_call(
        functools.partial(_mm_res_norm_body, nk=nk, keep_h=keep_h), grid=(m // tm, nk),
        in_specs=[pl.BlockSpec((tm, tk), lambda i, k: (i, k)),
                  pl.BlockSpec((None, tk, n), lambda i, k: (layer, k, 0)),
                  row, pl.BlockSpec((1, n), lambda i, k: (0, 0))],
        out_specs=out_specs, out_shape=out_shape,
        scratch_shapes=[] if keep_h else [pltpu.VMEM((tm, n), F32)],
        compiler_params=_params("parallel", "arbitrary"), name="matmul_res_norm",
    )(a, w, res, gain.reshape(1, n).astype(F32))
    return outs if keep_h else (None, outs[0])


def _mm_cat_norm_body(a1_ref, a2_ref, w1_ref, w2_ref, res_ref, g_ref, h_ref, n_ref):
    y = _dot(a1_ref[...], w1_ref[...]) + _dot(a2_ref[...], w2_ref[...]) + res_ref[...]
    h_ref[...] = y
    n_ref[...] = _norm_rows(y, g_ref[...]).astype(n_ref.dtype)


def _matmul_cat_norm(a1, a2, w, layer, res, gain, tm=512):
    m, k1 = a1.shape
    k2 = a2.shape[1]
    n = w.shape[2]
    tm = min(tm, m)
    assert m % tm == 0 and w.shape[1] == k1 + k2 and k1 % k2 == 0
    row = pl.BlockSpec((tm, n), lambda i: (i, 0))
    return pl.pallas_call(
        _mm_cat_norm_body, grid=(m // tm,),
        in_specs=[pl.BlockSpec((tm, k1), lambda i: (i, 0)),
                  pl.BlockSpec((tm, k2), lambda i: (i, 0)),
                  pl.BlockSpec((None, k1, n), lambda i: (layer, 0, 0)),
                  pl.BlockSpec((None, k2, n), lambda i: (layer, k1 // k2, 0)),
                  row, pl.BlockSpec((1, n), lambda i: (0, 0))],
        out_specs=[row, row],
        out_shape=[jax.ShapeDtypeStruct((m, n), F32), jax.ShapeDtypeStruct((m, n), BF16)],
        compiler_params=_params("parallel"), name="matmul_cat_norm",
    )(a1, a2, w, w, res, gain.reshape(1, n).astype(F32))


def _mem_attn_body(q_ref, k_ref, v_ref, o_ref):
    scale = HEAD_DIM ** -0.5
    for h in range(MEM_HEADS):
        sl = slice(h * HEAD_DIM, (h + 1) * HEAD_DIM)
        logits = _dot_nt(q_ref[0, :, sl], k_ref[0, :, sl]) * scale
        mx = jnp.max(logits, axis=-1, keepdims=True)
        e = jnp.exp(logits - mx)
        p = e / jnp.sum(e, axis=-1, keepdims=True)
        o_ref[0, :, sl] = _dot(p.astype(BF16), v_ref[0, :, sl]).astype(o_ref.dtype)


def _mem_attention(q, mem_k, mem_v, ts=1024):
    bsz, seq, _ = q.shape
    mlen = mem_k.shape[1]
    ts = min(ts, seq)
    return pl.pallas_call(
        _mem_attn_body, grid=(bsz, seq // ts),
        in_specs=[pl.BlockSpec((1, ts, MEM_WIDTH), lambda b, i: (b, i, 0)),
                  pl.BlockSpec((1, mlen, MEM_WIDTH), lambda b, i: (b, 0, 0)),
                  pl.BlockSpec((1, mlen, MEM_WIDTH), lambda b, i: (b, 0, 0))],
        out_specs=pl.BlockSpec((1, ts, MEM_WIDTH), lambda b, i: (b, i, 0)),
        out_shape=jax.ShapeDtypeStruct((bsz, seq, MEM_WIDTH), BF16),
        compiler_params=_params("parallel", "parallel"), name="mem_attention",
    )(q, mem_k, mem_v)


def _s5_build_body(lre_ref, lim_ref, ldt_ref, cre_ref, cim_ref, btr_ref, bti_ref,
                   tt_ref, wbr_ref, wbi_ref, car_ref, cai_ref, al_ref, r_ref, *, chunk):
    grp, pst = S5_GROUP, S5_STATE
    l_re, l_im = lre_ref[0], lim_ref[0]
    dt = jnp.exp(ldt_ref[0])
    x_re, x_im = l_re * dt, l_im * dt
    a_mag = jnp.exp(x_re)
    a_re, a_im = a_mag * jnp.cos(x_im), a_mag * jnp.sin(x_im)
    den = l_re * l_re + l_im * l_im
    z_re = ((a_re - 1.0) * l_re + a_im * l_im) / den
    z_im = (a_im * l_re - (a_re - 1.0) * l_im) / den
    bt_re, bt_im = btr_ref[0], bti_ref[0]
    bb_re = z_re * bt_re - z_im * bt_im
    bb_im = z_re * bt_im + z_im * bt_re
    c_re, c_im = cre_ref[0], cim_ref[0]

    sub = 8
    nhi = chunk // sub

    def closed_form(t):
        mag = jnp.exp(x_re[None] * t)
        ang = x_im[None] * t
        return mag * jnp.cos(ang), mag * jnp.sin(ang)

    def powers(reverse):
        a = lax.broadcasted_iota(jnp.int32, (nhi, 1, pst), 0)
        b = lax.broadcasted_iota(jnp.int32, (sub, 1, pst), 0)
        if reverse:
            a, b = nhi - 1 - a, sub - 1 - b
        hi_re, hi_im = closed_form((a * sub).astype(F32))
        lo_re, lo_im = closed_form(b.astype(F32))
        p_re = hi_re[:, None] * lo_re[None] - hi_im[:, None] * lo_im[None]
        p_im = hi_re[:, None] * lo_im[None] + hi_im[:, None] * lo_re[None]
        return p_re.reshape(chunk, 1, pst), p_im.reshape(chunk, 1, pst)

    def times(m_re, m_im, p_re, p_im):
        o_re = m_re[None] * p_re - m_im[None] * p_im
        o_im = m_re[None] * p_im + m_im[None] * p_re
        return o_re.reshape(chunk * grp, pst), o_im.reshape(chunk * grp, pst)

    p0_re, p0_im = powers(False)
    p1_re = p0_re * a_re[None] - p0_im * a_im[None]
    p1_im = p0_re * a_im[None] + p0_im * a_re[None]
    ca_re, ca_im = times(c_re, c_im, p1_re, p1_im)
    car_ref[0] = ca_re.astype(car_ref.dtype)
    cai_ref[0] = ca_im.astype(cai_ref.dtype)
    pr_re, pr_im = powers(True)
    wb_re, wb_im = times(bb_re, bb_im, pr_re, pr_im)
    wbr_ref[0] = wb_re.astype(wbr_ref.dtype)
    wbi_ref[0] = wb_im.astype(wbi_ref.dtype)
    lf = float(chunk)
    magl = jnp.exp(x_re * lf)
    al_ref[0] = jnp.concatenate([magl * jnp.cos(x_im * lf), magl * jnp.sin(x_im * lf)], axis=0)
    ce_re, ce_im = times(c_re, c_im, p0_re, p0_im)
    r_ref[...] = _dot_nt(bb_re, ce_re, HI) - _dot_nt(bb_im, ce_im, HI)
    width = chunk * grp
    lane = lax.broadcasted_iota(jnp.int32, (grp, width), 1)
    for j in range(chunk):
        r = r_ref[...]
        if j:
            r = jnp.where(lane >= j * grp, pltpu.roll(r, j * grp, axis=1), 0.0)
        tt_ref[0, j * grp:(j + 1) * grp, :] = r.astype(tt_ref.dtype)


def _s5_build(lam_re, lam_im, log_dt, b_re, b_im, c_re, c_im, chunk):
    ng, pst = lam_re.shape
    grp = S5_GROUP
    width = chunk * grp
    row = lambda x: x.astype(F32).reshape(ng, 1, pst)
    ldt = jnp.broadcast_to(log_dt.astype(F32)[:, None, None], (ng, 1, pst))
    args = (row(lam_re), row(lam_im), ldt, c_re.astype(F32), c_im.astype(F32),
            jnp.swapaxes(b_re.astype(F32), 1, 2), jnp.swapaxes(b_im.astype(F32), 1, 2))
    spec_row = pl.BlockSpec((1, 1, pst), lambda g: (g, 0, 0))
    spec_gp = pl.BlockSpec((1, grp, pst), lambda g: (g, 0, 0))
    spec_w = pl.BlockSpec((1, width, pst), lambda g: (g, 0, 0))
    return pl.pallas_call(
        functools.partial(_s5_build_body, chunk=chunk), grid=(ng,),
        in_specs=[spec_row, spec_row, spec_row, spec_gp, spec_gp, spec_gp, spec_gp],
        out_specs=[pl.BlockSpec((1, width, width), lambda g: (g, 0, 0)),
                   spec_w, spec_w, spec_w, spec_w,
                   pl.BlockSpec((1, 2, pst), lambda g: (g, 0, 0))],
        out_shape=[jax.ShapeDtypeStruct((ng, width, width), BF16),
                   jax.ShapeDtypeStruct((ng, width, pst), BF16),
                   jax.ShapeDtypeStruct((ng, width, pst), BF16),
                   jax.ShapeDtypeStruct((ng, width, pst), BF16),
                   jax.ShapeDtypeStruct((ng, width, pst), BF16),
                   jax.ShapeDtypeStruct((ng, 2, pst), F32)],
        scratch_shapes=[pltpu.VMEM((grp, width), F32)],
        compiler_params=_params("parallel"), name="s5_build",
    )(*args)


S5_OCT = LANES // S5_GROUP


S5_GPS = 2


def _s5_scan_body(x_ref, tt_ref, wbr_ref, wbi_ref, car_ref, cai_ref, al_ref, d_ref, z_ref,
                  acc_ref, xw_ref, ug_ref, sre_ref, sim_ref, *, bsz, nc, chunk):
    grp = S5_GROUP
    step_id = pl.program_id(1)
    rows = bsz * nc
    groups = [step_id * S5_GPS + i for i in range(S5_GPS)]
    lane_grp = lax.broadcasted_iota(jnp.int32, (rows // 2, LANES), 1) // grp

    def shift(to_grp, from_grp):
        return lax.rem((to_grp - from_grp) * grp + LANES, LANES)

    def words(t):
        return pltpu.bitcast(t.astype(BF16), jnp.uint32)

    @pl.when(step_id == 0)
    def _():
        acc_ref[...] = jnp.zeros_like(acc_ref)
        for j in range(chunk):
            xw_ref[j] = words(x_ref[j])

    for i, g in enumerate(groups):
        for col in range(chunk // S5_OCT):
            packed = jnp.zeros((rows // 2, LANES), jnp.uint32)
            for jj in range(S5_OCT):
                rolled = pltpu.roll(xw_ref[col * S5_OCT + jj], shift(jj, g), axis=1)
                packed = jnp.where(lane_grp == jj, rolled, packed)
            ug_ref[i, :, col * LANES:(col + 1) * LANES] = pltpu.bitcast(packed, BF16)

    for i in range(S5_GPS):
        sre_ref[i] = _dot(ug_ref[i], wbr_ref[i])
        sim_ref[i] = _dot(ug_ref[i], wbi_ref[i])

    def step(c, carry):
        out = []
        for i in range(S5_GPS):
            a_re, a_im = al_ref[i, 0:1, :], al_ref[i, 1:2, :]
            for b in range(bsz):
                h_re, h_im = carry[i * bsz + b]
                r = b * nc + c
                loc_re, loc_im = sre_ref[i, pl.ds(r, 1), :], sim_ref[i, pl.ds(r, 1), :]
                sre_ref[i, pl.ds(r, 1), :] = h_re
                sim_ref[i, pl.ds(r, 1), :] = h_im
                out.append((a_re * h_re - a_im * h_im + loc_re, a_re * h_im + a_im * h_re + loc_im))
        return tuple(out)

    zero = jnp.zeros((1, S5_STATE), F32)
    lax.fori_loop(0, nc, step, tuple((zero, zero) for _ in range(S5_GPS * bsz)))
    ys = []
    for i in range(S5_GPS):
        y = _dot(ug_ref[i], tt_ref[i])
        y = y + _dot_nt(sre_ref[i].astype(BF16), car_ref[i])
        ys.append(y - _dot_nt(sim_ref[i].astype(BF16), cai_ref[i]))

    for k in range(chunk):
        col = k // S5_OCT
        merged = acc_ref[k]
        for g, y in zip(groups, ys):
            rolled = pltpu.roll(words(y[:, col * LANES:(col + 1) * LANES]), shift(g, k % S5_OCT), axis=1)
            merged = jnp.where(lane_grp == g, rolled, merged)
        acc_ref[k] = merged

    @pl.when(step_id == S5_OCT // S5_GPS - 1)
    def _():
        for k in range(chunk):
            yk = pltpu.bitcast(acc_ref[k], BF16).astype(F32)
            z_ref[k] = jax.nn.gelu(yk + d_ref[...] * x_ref[k]).astype(z_ref.dtype)


def _s5_scan(x, tables, d_skip, bsz):
    tt, wbr, wbi, car, cai, al = tables
    chunk, rows, width = x.shape
    pst = S5_STATE
    tw = chunk * S5_GROUP
    steps = S5_OCT // S5_GPS
    tab = lambda o, t: (o * steps + t, 0, 0)
    spec_w = pl.BlockSpec((S5_GPS, tw, pst), tab)
    blk = pl.BlockSpec((chunk, rows, LANES), lambda o, t: (0, 0, o))
    words = pltpu.VMEM((chunk, rows // 2, LANES), jnp.uint32)
    return pl.pallas_call(
        functools.partial(_s5_scan_body, bsz=bsz, nc=rows // bsz, chunk=chunk),
        grid=(width // LANES, steps),
        in_specs=[blk, pl.BlockSpec((S5_GPS, tw, tw), tab), spec_w, spec_w, spec_w, spec_w,
                  pl.BlockSpec((S5_GPS, 2, pst), tab),
                  pl.BlockSpec((1, LANES), lambda o, t: (0, o))],
        out_specs=blk,
        out_shape=jax.ShapeDtypeStruct((chunk, rows, width), BF16),
        scratch_shapes=[words, words, pltpu.VMEM((S5_GPS, rows, tw), BF16),
                        pltpu.VMEM((S5_GPS, rows, pst), F32), pltpu.VMEM((S5_GPS, rows, pst), F32)],
        compiler_params=pltpu.CompilerParams(dimension_semantics=("parallel", "arbitrary"),
                                             vmem_limit_bytes=S5_SCAN_VMEM_BYTES),
        name="s5_scan",
    )(x, tt, wbr, wbi, car, cai, al, d_skip.reshape(1, width).astype(F32))


def _to_position_major(t, chunk):
    m, width = t.shape
    return t.reshape(m // chunk, chunk, width).transpose(1, 0, 2).reshape(m, width)


def _from_position_major(t, chunk):
    m, width = t.shape
    return t.reshape(chunk, m // chunk, width).transpose(1, 0, 2).reshape(m, width)


def _s5_mixer(u_pm, lam_re, lam_im, log_dt, b_re, b_im, c_re, c_im, d_skip, w_glu, layer, b_glu,
              bsz, chunk):
    m, width = u_pm.shape
    tables = _s5_build(lam_re, lam_im, log_dt, b_re, b_im, c_re, c_im, chunk)
    z = _s5_scan(u_pm.reshape(chunk, m // chunk, width), tables, d_skip, bsz).reshape(m, width)
    return _matmul(z, w_glu, layer=layer, out_dtype=BF16, glu_z=z, glu_bias=b_glu, tn=min(width, 768))


def _gdn_conv_body(x_ref, halo_ref, w_ref, o_ref, buf_ref, *, ts, nh):
    i = pl.program_id(1)
    j = pl.program_id(2)
    halo = halo_ref[0]
    buf_ref[0:8, :] = jnp.where(i == 0, jnp.zeros_like(halo), halo)
    buf_ref[8:8 + ts, :] = x_ref[0]
    acc = None
    for tap in range(GDN_CONV):
        off = 8 - (GDN_CONV - 1) + tap
        term = buf_ref[off:off + ts, :] * w_ref[tap:tap + 1, :]
        acc = term if acc is None else acc + term
    y = acc * _sigmoid(acc)
    qscale = jnp.where(j == 0, HEAD_DIM ** -0.5, 1.0).astype(F32)
    for h in range(nh):
        sl = slice(h * HEAD_DIM, (h + 1) * HEAD_DIM)
        yh = y[:, sl]
        nrm = yh * lax.rsqrt(jnp.sum(yh * yh, axis=-1, keepdims=True) + EPS) * qscale
        o_ref[0, :, sl] = jnp.where(j == 2, yh, nrm)


def _gdn_conv(x, conv_w, ts=256):
    bsz, seq, w4 = x.shape
    wd = w4 // 4
    w3 = 3 * wd
    ts = min(ts, seq)
    per8 = ts // 8
    return pl.pallas_call(
        functools.partial(_gdn_conv_body, ts=ts, nh=wd // HEAD_DIM),
        grid=(bsz, seq // ts, 3),
        in_specs=[pl.BlockSpec((1, ts, wd), lambda b, i, j: (b, i, j)),
                  pl.BlockSpec((1, 8, wd), lambda b, i, j: (b, jnp.maximum(i * per8 - 1, 0), j)),
                  pl.BlockSpec((GDN_CONV, wd), lambda b, i, j: (0, j))],
        out_specs=pl.BlockSpec((1, ts, wd), lambda b, i, j: (b, i, j)),
        out_shape=jax.ShapeDtypeStruct((bsz, seq, w3), F32),
        scratch_shapes=[pltpu.VMEM((ts + 8, wd), F32)],
        compiler_params=_params("parallel", "parallel", "parallel"), name="gdn_conv",
    )(x, x, conv_w.astype(F32))


_G_GC, _G_EGC, _G_EDEC, _G_BETA, _G_GLAST = 0, 16, 32, 48, 64


def _gdn_gates_body(x_ref, alog_ref, dtb_ref, o_ref, *, ts):
    x = x_ref[...]
    g = -jnp.exp(alog_ref[...]) * _softplus(x + dtb_ref[...])
    r = lax.broadcasted_iota(jnp.int32, (ts, ts), 0)
    c = lax.broadcasted_iota(jnp.int32, (ts, ts), 1)
    same = (r // GDN_CHUNK) == (c // GDN_CHUNK)
    gc = _dot(jnp.where(same & (r >= c), 1.0, 0.0).astype(F32), g, HI)
    gl = _dot(jnp.where(same, 1.0, 0.0).astype(F32), g, HI)
    lane = lax.broadcasted_iota(jnp.int32, x.shape, 1)
    out = jnp.where(lane < _G_EGC, gc,
          jnp.where(lane < _G_EDEC, jnp.exp(gc),
          jnp.where(lane < _G_BETA, jnp.exp(gl - gc),
          jnp.where(lane < _G_GLAST, _sigmoid(x), jnp.exp(gl)))))
    o_ref[...] = out


def _gdn_gates(ab, a_log, dt_bias, ts=512):
    m = ab.shape[0]
    ts = min(ts, m)
    nh = a_log.shape[0]

    def lanes(p):
        row = jnp.zeros((LANES,), F32)
        for off in (_G_GC, _G_EGC, _G_EDEC, _G_GLAST):
            row = row.at[off:off + nh].set(p.astype(F32))
        return row.reshape(1, LANES)

    spec = pl.BlockSpec((ts, LANES), lambda i: (i, 0))
    prm = pl.BlockSpec((1, LANES), lambda i: (0, 0))
    return pl.pallas_call(
        functools.partial(_gdn_gates_body, ts=ts), grid=(m // ts,),
        in_specs=[spec, prm, prm], out_specs=spec,
        out_shape=jax.ShapeDtypeStruct((m, LANES), F32),
        compiler_params=_params("parallel"), name="gdn_gates",
    )(ab, lanes(a_log), lanes(dt_bias))


def _split3(x):
    hi = x.astype(BF16)
    lo = (x - hi.astype(F32)).astype(BF16)
    return hi, lo


def _dot3(a, b):
    a_hi, a_lo = _split3(a)
    b_hi, b_lo = _split3(b)
    lhs = jnp.concatenate([a_hi, a_lo, a_hi], axis=1)
    rhs = jnp.concatenate([b_hi, b_hi, b_lo], axis=0)
    return _dot(lhs, rhs)


def _gdn_intra_body(q_ref, k_ref, v_ref, gp_ref, gct_ref, u_ref, w_ref, qd_ref, kd_ref, at_ref,
                    *, ngrp, grows):
    cl = GDN_CHUNK
    h = pl.program_id(1)
    gp = gp_ref[0]
    gc_col = _lane_pick(gp, h + _G_GC)
    egc_col = _lane_pick(gp, h + _G_EGC)
    edec_col = _lane_pick(gp, h + _G_EDEC)
    beta_col = _lane_pick(gp, h + _G_BETA)
    r = lax.broadcasted_iota(jnp.int32, (cl, cl), 0)
    c = lax.broadcasted_iota(jnp.int32, (cl, cl), 1)
    steps = int(math.log2(cl))
    nchunks = ngrp * grows // cl
    ps, sols = [], []
    for n in range(nchunks):
        rows = slice(n * cl, (n + 1) * cl)
        q, k, v = q_ref[0, rows, :], k_ref[0, rows, :], v_ref[0, rows, :]
        beta, egc = beta_col[rows], egc_col[rows]
        gc_row = gct_ref[0, 0, :, rows]
        decay = jnp.exp(jnp.where(r >= c, gc_col[rows] - gc_row, -jnp.inf))
        kb, vb = k * beta, v * beta
        kbf = k.astype(BF16)
        ps.append(-jnp.where(r > c, _dot_nt(kb.astype(BF16), kbf) * decay, 0.0))
        sols.append(jnp.concatenate([vb, kb * egc], axis=-1))
        qd_ref[0, rows, :] = (q * egc).astype(qd_ref.dtype)
        kd_ref[0, rows, :] = (k * edec_col[rows]).astype(kd_ref.dtype)
        at_ref[0, 0, rows, :] = jnp.where(r >= c, _dot_nt(q.astype(BF16), kbf) * decay,
                                          0.0).astype(at_ref.dtype)
    for s in range(steps):
        for n in range(nchunks):
            sols[n] = sols[n] + _dot3(ps[n], sols[n])
            if s + 1 < steps:
                ps[n] = _dot3(ps[n], ps[n])
    for n in range(nchunks):
        rows = slice(n * cl, (n + 1) * cl)
        u_ref[0, rows, :] = sols[n][:, :HEAD_DIM]
        w_ref[0, rows, :] = sols[n][:, HEAD_DIM:].astype(w_ref.dtype)


def _gdn_intra(qkv, gates, gct, ngrp=4, grows=256):
    bsz, seq, w3 = qkv.shape
    wd = w3 // 3
    nh = wd // HEAD_DIM
    cl = GDN_CHUNK
    grows = min(grows, seq)
    ngrp = min(ngrp, seq // grows)
    tt = ngrp * grows
    col = lambda off: pl.BlockSpec((1, tt, HEAD_DIM), lambda b, h, i: (b, i, h + off))
    tok_sd = lambda dt: jax.ShapeDtypeStruct((bsz, seq, wd), dt)
    return pl.pallas_call(
        functools.partial(_gdn_intra_body, ngrp=ngrp, grows=grows), grid=(bsz, nh, seq // tt),
        in_specs=[col(0), col(nh), col(2 * nh),
                  pl.BlockSpec((1, tt, LANES), lambda b, h, i: (b, i, 0)),
                  pl.BlockSpec((1, 1, 1, tt), lambda b, h, i: (b, h, 0, i))],
        out_specs=[col(0), col(0), col(0), col(0),
                   pl.BlockSpec((1, 1, tt, cl), lambda b, h, i: (b, h, i, 0))],
        out_shape=[tok_sd(F32), tok_sd(BF16), tok_sd(BF16), tok_sd(BF16),
                   jax.ShapeDtypeStruct((bsz, nh, seq, cl), BF16)],
        compiler_params=_params("parallel", "parallel", "parallel"), name="gdn_intra",
    )(qkv, qkv, qkv, gates, gct)


def _gdn_scan_body(gl_ref, u_ref, w_ref, qd_ref, kd_ref, at_ref, gate_ref, on_ref, o_ref,
                   state_ref, *, nb, nh):
    cl = GDN_CHUNK
    b = pl.program_id(0)
    i = pl.program_id(1)

    @pl.when(i == 0)
    def _():
        state_ref[...] = jnp.zeros_like(state_ref)

    for n in range(nb):
        rows = slice(n * cl, (n + 1) * cl)
        from_state = []
        for h in range(nh):
            sl = slice(h * HEAD_DIM, (h + 1) * HEAD_DIM)
            wq = jnp.concatenate([w_ref[0, rows, sl], qd_ref[0, rows, sl]], axis=0).astype(BF16)
            from_state.append(_dot(wq, state_ref[h].astype(BF16)))
        for h in range(nh):
            sl = slice(h * HEAD_DIM, (h + 1) * HEAD_DIM)
            v_new = u_ref[0, rows, sl] - from_state[h][:cl]
            vb = v_new.astype(BF16)
            out = from_state[h][cl:] + _dot(at_ref[0, h, rows, :].astype(BF16), vb)
            g_last = gl_ref[b, i * nb + n, h]
            state_ref[h] = state_ref[h] * g_last + _dot_tn(kd_ref[0, rows, sl].astype(BF16), vb)
            gate = gate_ref[0, rows, sl]
            nrm = out * lax.rsqrt(jnp.mean(out * out, axis=-1, keepdims=True) + EPS) * on_ref[...]
            o_ref[0, rows, sl] = (nrm * (gate * _sigmoid(gate))).astype(o_ref.dtype)


def _gdn_scan(glast, u_c, w_c, q_dec, k_dec, attn, main, o_norm, nb=4):
    bsz, seq, wd = u_c.shape
    nh = wd // HEAD_DIM
    cl = GDN_CHUNK
    nb = min(nb, seq // cl)
    tt = nb * cl
    tok = pl.BlockSpec((1, tt, wd), lambda b, i: (b, i, 0))
    return pl.pallas_call(
        functools.partial(_gdn_scan_body, nb=nb, nh=nh), grid=(bsz, seq // tt),
        in_specs=[pl.BlockSpec(memory_space=pltpu.SMEM), tok, tok, tok, tok,
                  pl.BlockSpec((1, nh, tt, cl), lambda b, i: (b, 0, i, 0)),
                  pl.BlockSpec((1, tt, wd), lambda b, i: (b, i, 3)),
                  pl.BlockSpec((1, HEAD_DIM), lambda b, i: (0, 0))],
        out_specs=tok,
        out_shape=jax.ShapeDtypeStruct((bsz, seq, wd), BF16),
        scratch_shapes=[pltpu.VMEM((nh, HEAD_DIM, HEAD_DIM), F32)],
        compiler_params=_params("parallel", "arbitrary"), name="gdn_scan",
    )(glast, u_c, w_c, q_dec, k_dec, attn, main, o_norm.reshape(1, HEAD_DIM).astype(F32))


def _gdn_mixer(main, ab, conv_w, a_log, dt_bias, o_norm):
    bsz, seq, w4 = main.shape
    nh = w4 // 4 // HEAD_DIM
    qkv = _gdn_conv(main, conv_w)
    gates = _gdn_gates(ab, a_log, dt_bias).reshape(bsz, seq, LANES)
    gct = gates[:, :, _G_GC:_G_GC + nh].transpose(0, 2, 1).reshape(bsz, nh, 1, seq)
    glast = gates[:, GDN_CHUNK - 1::GDN_CHUNK, _G_GLAST:_G_GLAST + nh]
    u_c, w_c, q_dec, k_dec, attn = _gdn_intra(qkv, gates, gct)
    return _gdn_scan(glast, u_c, w_c, q_dec, k_dec, attn, main, o_norm)


def _fox_cumf_body(x_ref, bf_ref, o_ref, carry_ref, *, ts):
    @pl.when(pl.program_id(1) == 0)
    def _():
        carry_ref[...] = jnp.zeros_like(carry_ref)

    ls = -_softplus(-(x_ref[0] + bf_ref[...]))
    r = lax.broadcasted_iota(jnp.int32, (ts, ts), 0)
    c = lax.broadcasted_iota(jnp.int32, (ts, ts), 1)
    cum = _dot(jnp.where(r >= c, 1.0, 0.0).astype(F32), ls, HI) + carry_ref[...]
    o_ref[0] = cum
    carry_ref[...] = cum[ts - 1:ts, :]


def _fox_cumf(f_logit, b_f, ts=256):
    bsz, seq, _ = f_logit.shape
    ts = min(ts, seq)
    bf = jnp.zeros((LANES,), F32).at[:b_f.shape[0]].set(b_f.astype(F32)).reshape(1, LANES)
    spec = pl.BlockSpec((1, ts, LANES), lambda b, i: (b, i, 0))
    return pl.pallas_call(
        functools.partial(_fox_cumf_body, ts=ts), grid=(bsz, seq // ts),
        in_specs=[spec, pl.BlockSpec((1, LANES), lambda b, i: (0, 0))],
        out_specs=spec, out_shape=jax.ShapeDtypeStruct((bsz, seq, LANES), F32),
        scratch_shapes=[pltpu.VMEM((1, LANES), F32)],
        compiler_params=_params("parallel", "arbitrary"), name="fox_cumf",
    )(f_logit, bf)


LOG2E = math.log2(math.e)
FOX_QSCALE = HEAD_DIM ** -0.5 * LOG2E


def _fox_attn_body(q_ref, k_ref, v_ref, cft_ref, o_ref, *scratch, tq, nblk):
    qi = pl.program_id(2)
    tb = tq // nblk
    m_refs, l_refs, acc_refs = scratch[0::3], scratch[1::3], scratch[2::3]
    for blk in range(nblk):
        m_refs[blk][...] = jnp.full_like(m_refs[blk], -jnp.inf)
        l_refs[blk][...] = jnp.zeros_like(l_refs[blk])
        acc_refs[blk][...] = jnp.zeros_like(acc_refs[blk])

    def scores(blk, kb):
        return _dot_nt(q_ref[0, blk * tb:(blk + 1) * tb, :], kb)

    def softmax_step(blk, s, ck, masked):
        m_ref, l_ref = m_refs[blk], l_refs[blk]
        s = s - ck
        if masked:
            r = lax.broadcasted_iota(jnp.int32, s.shape, 0) + blk * tb
            c = lax.broadcasted_iota(jnp.int32, s.shape, 1)
            s = jnp.where(r >= c, s, -jnp.inf)
        m_prev = m_ref[...]
        m_new = jnp.maximum(m_prev, jnp.max(s, axis=-1, keepdims=True))
        alpha = jnp.exp2(m_prev - m_new)
        p = jnp.exp2(s - m_new)
        l_ref[...] = alpha * l_ref[...] + jnp.sum(p, axis=-1, keepdims=True)
        m_ref[...] = m_new
        return p.astype(BF16), alpha

    def accumulate(blk, p, alpha, vb):
        acc_refs[blk][...] = alpha * acc_refs[blk][...] + _dot(p, vb)

    def key_tiles(first, count):
        tiles = []
        for t in range(count):
            start = pl.multiple_of((first + t) * tq, tq)
            tiles.append((k_ref[0, pl.ds(start, tq), :], v_ref[0, pl.ds(start, tq), :],
                          cft_ref[0, 0, first + t] * LOG2E))
        s_all = [[scores(blk, kb) for blk in range(nblk)] for kb, _, _ in tiles]
        for t, (_, vb, ck) in enumerate(tiles):
            for blk in range(nblk):
                p, alpha = softmax_step(blk, s_all[t][blk], ck, False)
                accumulate(blk, p, alpha, vb)

    def body(j, carry):
        key_tiles(2 * j, 2)
        return carry

    lax.fori_loop(0, qi // 2, body, 0)

    @pl.when(qi % 2 == 1)
    def _():
        key_tiles(qi - 1, 1)

    d0 = pl.multiple_of(qi * tq, tq)
    ck = cft_ref[0, 0, qi] * LOG2E
    nkeys = [(blk + 1) * tb for blk in range(nblk)]
    for blk in range(nblk):
        s_cur = scores(blk, k_ref[0, pl.ds(d0, nkeys[blk]), :])
        p, alpha = softmax_step(blk, s_cur, ck[:, :nkeys[blk]], True)
        accumulate(blk, p, alpha, v_ref[0, pl.ds(d0, nkeys[blk]), :])
        o_ref[0, blk * tb:(blk + 1) * tb, :] = (acc_refs[blk][...] / l_refs[blk][...]).astype(o_ref.dtype)


def _fox_attention(qkv, cumf_t, tq, nblk=4):
    bsz, seq, w3 = qkv.shape
    wd = w3 // 3
    nh = wd // HEAD_DIM
    nq = seq // tq
    tb = tq // nblk
    kv = lambda off: pl.BlockSpec((1, seq, HEAD_DIM), lambda b, h, qi: (b, 0, h + off))
    scratch = []
    for _ in range(nblk):
        scratch += [pltpu.VMEM((tb, 1), F32), pltpu.VMEM((tb, 1), F32), pltpu.VMEM((tb, HEAD_DIM), F32)]
    return pl.pallas_call(
        functools.partial(_fox_attn_body, tq=tq, nblk=nblk), grid=(bsz, nh, nq),
        in_specs=[pl.BlockSpec((1, tq, HEAD_DIM), lambda b, h, qi: (b, qi, h)),
                  kv(nh), kv(2 * nh),
                  pl.BlockSpec((1, 1, nq, 1, tq), lambda b, h, qi: (b, h, 0, 0, 0))],
        out_specs=pl.BlockSpec((1, tq, HEAD_DIM), lambda b, h, qi: (b, qi, h)),
        out_shape=jax.ShapeDtypeStruct((bsz, seq, wd), BF16),
        scratch_shapes=scratch,
        compiler_params=_params("parallel", "parallel", "arbitrary"),
        name="fox_attention",
    )(qkv, qkv, qkv, cumf_t)


def _fox_mixer(qkv, f_logit, b_f, tq=1024):
    bsz, seq, w3 = qkv.shape
    nh = w3 // 3 // HEAD_DIM
    tq = min(tq, seq)
    cumf = _fox_cumf(f_logit, b_f)
    cumf_t = cumf[:, :, :nh].transpose(0, 2, 1).reshape(bsz, nh, seq // tq, 1, tq)
    return _fox_attention(qkv, cumf_t, tq)


def _pad_cols(w, groups):
    out = jnp.zeros((w.shape[0], LANES), w.dtype)
    for off in groups:
        out = out.at[:, off:off + w.shape[1]].set(w)
    return out


def kernel(x, mem, mem_norm, w_mem_kv, norm1, w_out, norm2, w_up, w_down, norm_f,
           s5_w_in, s5_lam_re, s5_lam_im, s5_log_dt, s5_b_re, s5_b_im, s5_c_re, s5_c_im,
           s5_d_skip, s5_w_glu, s5_b_glu,
           gdn_w_in, gdn_conv_w, gdn_a_log, gdn_dt_bias, gdn_o_norm,
           fox_w_in, fox_b_f):
    bsz, seq, d = x.shape
    m = bsz * seq
    depth = norm1.shape[0]
    wd = d - MEM_WIDTH
    nh = wd // HEAD_DIM
    mlen = mem.shape[1]

    mem_a = _rmsnorm(mem.reshape(bsz * mlen, d), mem_norm)
    mkv = _matmul(mem_a, w_mem_kv.astype(BF16)[None], out_dtype=BF16, tn=512)
    mem_k = mkv[:, :MEM_WIDTH].reshape(bsz, mlen, MEM_WIDTH)
    mem_v = mkv[:, MEM_WIDTH:].reshape(bsz, mlen, MEM_WIDTH)

    w_out_b, w_down_b = w_out.astype(BF16), w_down.astype(BF16)
    s5_w_in_b, s5_w_glu_b = s5_w_in.astype(BF16), s5_w_glu.astype(BF16)

    h = x.reshape(m, d)
    a = _rmsnorm(h, norm1[0])
    out = None
    for i in range(depth):
        kind, j = i % 3, i // 3
        if kind == 0:
            chunk = min(S5_CHUNK, seq)
            u = _matmul(_to_position_major(a, chunk), s5_w_in_b, layer=j, n=wd, out_dtype=F32, tn=768)
            mq = _matmul(a, s5_w_in_b, layer=j, col0=wd // MEM_WIDTH, n=MEM_WIDTH, out_dtype=BF16,
                         tn=MEM_WIDTH)
            mix = _s5_mixer(u, s5_lam_re[j], s5_lam_im[j], s5_log_dt[j], s5_b_re[j], s5_b_im[j],
                            s5_c_re[j], s5_c_im[j], s5_d_skip[j], s5_w_glu_b, j, s5_b_glu[j],
                            bsz, chunk)
            mix = _from_position_major(mix, chunk)
        elif kind == 1:
            w_in = gdn_w_in[j]
            w_ab = (_pad_cols(w_in[:, 4 * wd:4 * wd + nh], (_G_GC, _G_EGC, _G_EDEC, _G_GLAST))
                    + _pad_cols(w_in[:, 4 * wd + nh:4 * wd + 2 * nh], (_G_BETA,)))
            main = _matmul_wcast(a, gdn_w_in, layer=j, n=4 * wd, out_dtype=F32, tn=768)
            ab = _matmul(a, w_ab.astype(BF16)[None], out_dtype=F32)
            mq = _matmul(a, w_in[:, -MEM_WIDTH:].astype(BF16)[None], out_dtype=BF16, tn=MEM_WIDTH)
            main = main.reshape(bsz, seq, 4 * wd)
            mix = _gdn_mixer(main, ab, gdn_conv_w[j], gdn_a_log[j], gdn_dt_bias[j],
                             gdn_o_norm[j]).reshape(m, wd)
        else:
            w_in = fox_w_in[j]
            qscale = jnp.concatenate([jnp.full((wd,), FOX_QSCALE, F32), jnp.ones((2 * wd,), F32)])
            qkv = _matmul_wcast(a, fox_w_in, layer=j, n=3 * wd, out_dtype=BF16, act="scale",
                                bias=qscale, tn=768)
            w_f = _pad_cols(w_in[:, 3 * wd:3 * wd + nh], (0,))
            fl = _matmul(a, w_f.astype(BF16)[None], out_dtype=F32)
            mq = _matmul(a, w_in[:, -MEM_WIDTH:].astype(BF16)[None], out_dtype=BF16, tn=MEM_WIDTH)
            mix = _fox_mixer(qkv.reshape(bsz, seq, 3 * wd), fl.reshape(bsz, seq, LANES),
                             fox_b_f[j]).reshape(m, wd)
        read = _mem_attention(mq.reshape(bsz, seq, MEM_WIDTH), mem_k, mem_v).reshape(m, MEM_WIDTH)
        h, a = _matmul_cat_norm(mix, read, w_out_b, i, h, norm2[i])
        up = _matmul_wcast(a, w_up, layer=i, out_dtype=BF16, act="relu2")
        if i + 1 < depth:
            h, a = _matmul_res_norm(up, w_down_b, i, h, norm1[i + 1], norm_dtype=BF16, keep_h=True)
        else:
            _, out = _matmul_res_norm(up, w_down_b, i, h, norm_f, norm_dtype=x.dtype, keep_h=False)
    return out.reshape(bsz, seq, d)
```
